```python
import math
import jax, jax.numpy as jnp
from jax import lax
import numpy as np

D_MODEL = 2048
BATCH = 1
SEQ = 8192
DEPTH = 2
DEC_BATCH = 128
DEC_SEQ = 4
PAST_LEN = 2048
PAGE_SIZE = 128

N_MIXERS = 2
N_ATTN_LAYERS = (DEPTH + 1) // 2
N_SSM_LAYERS = DEPTH // 2
N_META = 16
EPS = 1e-6
N_HEADS = 8
HEAD_DIM = 128
V_DIM = 2 * HEAD_DIM
QK_WIDTH = N_HEADS * 2 * HEAD_DIM
V_WIDTH = N_HEADS * V_DIM
ATTN_BLOCK = 128
SSM_GROUP = 16
N_GROUPS = D_MODEL // SSM_GROUP
STATE_DIM = 64
SSM_CHUNK = 128
N_KEYS = 128
N_EXPERTS = N_KEYS * N_KEYS
PEER_HEADS = 8
PEER_TOPK = 16
D_KEY = 256
HALF_KEY = D_KEY // 2
PEER_BLOCK = 128

kernel_name = 'hybrid_diffattn_s5_peer_step'


def rmsnorm(x, g):
    xf = x.astype(jnp.float32)
    y = xf * lax.rsqrt(jnp.mean(xf * xf, axis=-1, keepdims=True) + EPS)
    return (y * g.astype(jnp.float32)).astype(x.dtype)


def alibi_slopes():
    return jnp.asarray(2.0 ** (-8.0 * np.arange(1, N_HEADS + 1) / N_HEADS), dtype=jnp.float32)


def lambda_init(layer):
    return 0.8 - 0.6 * math.exp(-0.3 * layer)


def diff_lambda(lam_vecs, lam_init):
    lv = lam_vecs.astype(jnp.float32)
    return jnp.exp(jnp.sum(lv[0] * lv[1])) - jnp.exp(jnp.sum(lv[2] * lv[3])) + lam_init


def diff_attn_qkv(h, w_qkv):
    b, l, _ = h.shape
    qkv = h @ w_qkv
    q = qkv[..., :QK_WIDTH].reshape(b, l, N_HEADS, 2 * HEAD_DIM)
    k = qkv[..., QK_WIDTH:2 * QK_WIDTH].reshape(b, l, N_HEADS, 2 * HEAD_DIM)
    v = qkv[..., 2 * QK_WIDTH:].reshape(b, l, N_HEADS, V_DIM)
    return q, k, v


def diff_attn_core(q, k, v, q_pos, k_pos, lam):
    scale = HEAD_DIM ** -0.5
    dist = (q_pos[:, None] - k_pos[None, :]).astype(jnp.float32)
    bias = -alibi_slopes()[:, None, None] * dist
    causal = dist >= 0

    def probs(qc, kc):
        s = jnp.einsum('bqhd,bkhd->bhqk', qc, kc).astype(jnp.float32) * scale + bias
        s = jnp.where(causal, s, -1e30)
        return jax.nn.softmax(s, axis=-1)

    p = probs(q[..., :HEAD_DIM], k[..., :HEAD_DIM]) - lam * probs(q[..., HEAD_DIM:], k[..., HEAD_DIM:])
    return jnp.einsum('bhqk,bkhe->bqhe', p.astype(v.dtype), v)


def diff_attn_out(o, subln_g, w_o, lam_init):
    b, l = o.shape[:2]
    o = rmsnorm(o, subln_g) * (1.0 - lam_init)
    return o.reshape(b, l, V_WIDTH) @ w_o


def s5_discretize(lam_re, lam_im, log_dt, b_re, b_im):
    lr = lam_re.astype(jnp.float32)
    li = lam_im.astype(jnp.float32)
    dt = jnp.exp(log_dt.astype(jnp.float32))[:, None]
    mag = jnp.exp(lr * dt)
    a_re = mag * jnp.cos(li * dt)
    a_im = mag * jnp.sin(li * dt)
    den = lr * lr + li * li
    nr = a_re - 1.0
    coef_re = (nr * lr + a_im * li) / den
    coef_im = (a_im * lr - nr * li) / den
    br = b_re.astype(jnp.float32)
    bi = b_im.astype(jnp.float32)
    bb_re = coef_re[..., None] * br - coef_im[..., None] * bi
    bb_im = coef_re[..., None] * bi + coef_im[..., None] * br
    return a_re, a_im, bb_re, bb_im


def s5_combine(e1, e2):
    a1r, a1i, b1r, b1i = e1
    a2r, a2i, b2r, b2i = e2
    return (a1r * a2r - a1i * a2i,
            a1r * a2i + a1i * a2r,
            a2r * b1r - a2i * b1i + b2r,
            a2r * b1i + a2i * b1r + b2i)


def s5_segment(disc, c_re, c_im, d, u, s0_re, s0_im):
    a_re, a_im, bb_re, bb_im = disc
    bu_re = jnp.einsum('gpc,blgc->blgp', bb_re, u)
    bu_im = jnp.einsum('gpc,blgc->blgp', bb_im, u)
    a_re_b = jnp.broadcast_to(a_re, bu_re.shape)
    a_im_b = jnp.broadcast_to(a_im, bu_re.shape)
    ar, ai, xr, xi = lax.associative_scan(s5_combine, (a_re_b, a_im_b, bu_re, bu_im), axis=1)
    xr = xr + ar * s0_re[:, None] - ai * s0_im[:, None]
    xi = xi + ar * s0_im[:, None] + ai * s0_re[:, None]
    y = jnp.einsum('gcp,blgp->blgc', c_re, xr) - jnp.einsum('gcp,blgp->blgc', c_im, xi) + d * u
    return y, xr[:, -1], xi[:, -1]


def s5_mixer(h, w_in, lam_re, lam_im, log_dt, b_re, b_im, c_re, c_im, d, w_glu, s0_re, s0_im, n_lead):
    b, l, _ = h.shape
    u = (h @ w_in).astype(jnp.float32).reshape(b, l, N_GROUPS, SSM_GROUP)
    disc = s5_discretize(lam_re, lam_im, log_dt, b_re, b_im)
    cr = c_re.astype(jnp.float32)
    ci = c_im.astype(jnp.float32)
    dd = d.astype(jnp.float32)
    y, sr, si = s5_segment(disc, cr, ci, dd, u[:, :n_lead], s0_re.astype(jnp.float32), s0_im.astype(jnp.float32))
    n_rest = l - n_lead
    if n_rest > 0:
        n_chunks = n_rest // SSM_CHUNK
        uc = jnp.moveaxis(u[:, n_lead:].reshape(b, n_chunks, SSM_CHUNK, N_GROUPS, SSM_GROUP), 1, 0)

        def body(carry, u_chunk):
            y_c, r, i = s5_segment(disc, cr, ci, dd, u_chunk, carry[0], carry[1])
            return (r, i), y_c

        (sr, si), yc = lax.scan(body, (sr, si), uc)
        y_rest = jnp.moveaxis(yc, 0, 1).reshape(b, n_rest, N_GROUPS, SSM_GROUP)
        y = jnp.concatenate([y, y_rest], axis=1)
    y = jax.nn.gelu(y.reshape(b, l, D_MODEL), approximate=False).astype(h.dtype)
    z = y @ w_glu
    out = z[..., :D_MODEL] * jax.nn.sigmoid(z[..., D_MODEL:])
    return out, sr, si


def peer_ffn(h, w_q, sub_keys, u_tab, v_tab):
    lead = h.shape[:-1]
    x = h.reshape(-1, D_MODEL)
    n = x.shape[0]
    x = jnp.pad(x, ((0, (-n) % PEER_BLOCK), (0, 0)))
    xb = x.reshape(-1, PEER_BLOCK, D_MODEL)

    def block(xt):
        q = (xt @ w_q).reshape(PEER_BLOCK, PEER_HEADS, 2, HALF_KEY)
        s = jnp.einsum('nhcd,ckd->nhck', q, sub_keys).astype(jnp.float32)
        sv, si = lax.top_k(s, PEER_TOPK)
        cand_s = (sv[:, :, 0, :, None] + sv[:, :, 1, None, :]).reshape(PEER_BLOCK, PEER_HEADS, PEER_TOPK * PEER_TOPK)
        cand_i = (si[:, :, 0, :, None] * N_KEYS + si[:, :, 1, None, :]).reshape(PEER_BLOCK, PEER_HEADS, PEER_TOPK * PEER_TOPK)
        top_s, top_pos = lax.top_k(cand_s, PEER_TOPK)
        ids = jnp.take_along_axis(cand_i, top_pos, axis=-1)
        g = jax.nn.softmax(top_s, axis=-1)
        act = jax.nn.gelu(jnp.einsum('nd,nhkd->nhk', xt, u_tab[ids]).astype(jnp.float32), approximate=False)
        w = (g * act).astype(xt.dtype)
        return jnp.einsum('nhk,nhkd->nd', w, v_tab[ids])

    out = lax.map(block, xb).reshape(-1, D_MODEL)[:n]
    return out.reshape(*lead, D_MODEL)


def setup_inputs(seed: int = 0) -> dict:
    key = jax.random.key(seed)
    ks = jax.random.split(key, 32)

    def nrm(k, shape, scale):
        return jax.random.normal(k, shape, jnp.float32) * scale

    n_pages = PAST_LEN // PAGE_SIZE
    n_used = DEC_BATCH * n_pages
    n_phys = n_used + n_used // 4
    page_table = jax.random.permutation(ks[4], n_phys)[:n_used].reshape(DEC_BATCH, n_pages).astype(jnp.int32)
    lam_im0 = jnp.broadcast_to(jnp.pi * jnp.arange(STATE_DIM, dtype=jnp.float32), (N_SSM_LAYERS, N_GROUPS, STATE_DIM))
    return {
        'x_prompt': nrm(ks[0], (BATCH, SEQ, D_MODEL), 1.0),
        'x_sample': nrm(ks[1], (DEC_BATCH, DEC_SEQ, D_MODEL), 1.0),
        'cache_k': nrm(ks[2], (N_ATTN_LAYERS, n_phys, PAGE_SIZE, N_HEADS, 2 * HEAD_DIM), 1.0),
        'cache_v': nrm(ks[3], (N_ATTN_LAYERS, n_phys, PAGE_SIZE, N_HEADS, V_DIM), 1.0),
        'state_ssm_re': nrm(ks[5], (N_SSM_LAYERS, DEC_BATCH, N_GROUPS, STATE_DIM), 0.1),
        'state_ssm_im': nrm(ks[6], (N_SSM_LAYERS, DEC_BATCH, N_GROUPS, STATE_DIM), 0.1),
        'page_table': page_table,
        'meta_tokens': nrm(ks[7], (N_META, D_MODEL), 1.0),
        'norm_mix': 1.0 + nrm(ks[8], (DEPTH, D_MODEL), 0.01),
        'norm_ffn': 1.0 + nrm(ks[9], (DEPTH, D_MODEL), 0.01),
        'norm_final': 1.0 + nrm(ks[10], (D_MODEL,), 0.01),
        'attn_w_qkv': nrm(ks[11], (N_ATTN_LAYERS, D_MODEL, 2 * QK_WIDTH + V_WIDTH), D_MODEL ** -0.5),
        'attn_lambda': nrm(ks[12], (N_ATTN_LAYERS, 4, HEAD_DIM), 0.1),
        'attn_subln': 1.0 + nrm(ks[13], (N_ATTN_LAYERS, V_DIM), 0.01),
        'attn_w_o': nrm(ks[14], (N_ATTN_LAYERS, V_WIDTH, D_MODEL), V_WIDTH ** -0.5),
        'ssm_w_in': nrm(ks[15], (N_SSM_LAYERS, D_MODEL, D_MODEL), D_MODEL ** -0.5),
        'ssm_lambda_re': -0.5 + nrm(ks[16], (N_SSM_LAYERS, N_GROUPS, STATE_DIM), 0.01),
        'ssm_lambda_im': lam_im0 + nrm(ks[17], (N_SSM_LAYERS, N_GROUPS, STATE_DIM), 0.01),
        'ssm_log_dt': jax.random.uniform(ks[18], (N_SSM_LAYERS, N_GROUPS), jnp.float32, math.log(1e-3), math.log(1e-1)),
        'ssm_b_re': nrm(ks[19], (N_SSM_LAYERS, N_GROUPS, STATE_DIM, SSM_GROUP), (2 * SSM_GROUP) ** -0.5),
        'ssm_b_im': nrm(ks[20], (N_SSM_LAYERS, N_GROUPS, STATE_DIM, SSM_GROUP), (2 * SSM_GROUP) ** -0.5),
        'ssm_c_re': nrm(ks[21], (N_SSM_LAYERS, N_GROUPS, SSM_GROUP, STATE_DIM), STATE_DIM ** -0.5),
        'ssm_c_im': nrm(ks[22], (N_SSM_LAYERS, N_GROUPS, SSM_GROUP, STATE_DIM), STATE_DIM ** -0.5),
        'ssm_d': nrm(ks[23], (N_SSM_LAYERS, N_GROUPS, SSM_GROUP), 1.0),
        'ssm_w_glu': nrm(ks[24], (N_SSM_LAYERS, D_MODEL, 2 * D_MODEL), D_MODEL ** -0.5),
        'peer_w_q': nrm(ks[25], (DEPTH, D_MODEL, PEER_HEADS * D_KEY), D_MODEL ** -0.5),
        'peer_sub_keys': nrm(ks[26], (DEPTH, 2, N_KEYS, HALF_KEY), HALF_KEY ** -0.5),
        'peer_u': nrm(ks[27], (DEPTH, N_EXPERTS, D_MODEL), D_MODEL ** -0.5),
        'peer_v': nrm(ks[28], (DEPTH, N_EXPERTS, D_MODEL), PEER_HEADS ** -0.5),
    }


def reference(x_prompt, x_sample, cache_k, cache_v, state_ssm_re, state_ssm_im, page_table,
              meta_tokens, norm_mix, norm_ffn, norm_final,
              attn_w_qkv, attn_lambda, attn_subln, attn_w_o,
              ssm_w_in, ssm_lambda_re, ssm_lambda_im, ssm_log_dt, ssm_b_re, ssm_b_im,
              ssm_c_re, ssm_c_im, ssm_d, ssm_w_glu,
              peer_w_q, peer_sub_keys, peer_u, peer_v):
    meta = jnp.broadcast_to(meta_tokens.astype(x_prompt.dtype)[None], (BATCH, N_META, D_MODEL))
    xp = jnp.concatenate([meta, x_prompt], axis=1)
    xs = x_sample
    t_tot = xp.shape[1]
    pos_p = jnp.arange(t_tot)
    q_pos_s = PAST_LEN + jnp.arange(DEC_SEQ)
    k_pos_s = jnp.arange(PAST_LEN + DEC_SEQ)
    new_k_p, new_v_p, new_k_s, new_v_s = [], [], [], []
    new_sr_p, new_si_p, new_sr_s, new_si_s = [], [], [], []
    for i in range(DEPTH):
        j = i // N_MIXERS
        hp = rmsnorm(xp, norm_mix[i])
        hs = rmsnorm(xs, norm_mix[i])
        if i % N_MIXERS == 0:
            lam_init = lambda_init(i)
            lam = diff_lambda(attn_lambda[j], lam_init)
            qp, kp, vp = diff_attn_qkv(hp, attn_w_qkv[j])
            blocks = []
            for qb in range(0, t_tot, ATTN_BLOCK):
                qe = min(qb + ATTN_BLOCK, t_tot)
                blocks.append(diff_attn_core(qp[:, qb:qe], kp[:, :qe], vp[:, :qe], pos_p[qb:qe], pos_p[:qe], lam))
            op = jnp.concatenate(blocks, axis=1)
            qs_, ks_, vs_ = diff_attn_qkv(hs, attn_w_qkv[j])
            k_past = cache_k[j, page_table].reshape(DEC_BATCH, PAST_LEN, N_HEADS, 2 * HEAD_DIM).astype(ks_.dtype)
            v_past = cache_v[j, page_table].reshape(DEC_BATCH, PAST_LEN, N_HEADS, V_DIM).astype(vs_.dtype)
            os_ = diff_attn_core(qs_, jnp.concatenate([k_past, ks_], axis=1), jnp.concatenate([v_past, vs_], axis=1),
                                 q_pos_s, k_pos_s, lam)
            xp = xp + diff_attn_out(op, attn_subln[j], attn_w_o[j], lam_init)
            xs = xs + diff_attn_out(os_, attn_subln[j], attn_w_o[j], lam_init)
            new_k_p.append(kp)
            new_v_p.append(vp)
            new_k_s.append(ks_)
            new_v_s.append(vs_)
        else:
            zeros = jnp.zeros((BATCH, N_GROUPS, STATE_DIM), jnp.float32)
            mp, sr_p, si_p = s5_mixer(hp, ssm_w_in[j], ssm_lambda_re[j], ssm_lambda_im[j], ssm_log_dt[j],
                                      ssm_b_re[j], ssm_b_im[j], ssm_c_re[j], ssm_c_im[j], ssm_d[j], ssm_w_glu[j],
                                      zeros, zeros, N_META)
            ms, sr_s, si_s = s5_mixer(hs, ssm_w_in[j], ssm_lambda_re[j], ssm_lambda_im[j], ssm_log_dt[j],
                                      ssm_b_re[j], ssm_b_im[j], ssm_c_re[j], ssm_c_im[j], ssm_d[j], ssm_w_glu[j],
                                      state_ssm_re[j], state_ssm_im[j], DEC_SEQ)
            xp = xp + mp
            xs = xs + ms
            new_sr_p.append(sr_p.astype(state_ssm_re.dtype))
            new_si_p.append(si_p.astype(state_ssm_im.dtype))
            new_sr_s.append(sr_s.astype(state_ssm_re.dtype))
            new_si_s.append(si_s.astype(state_ssm_im.dtype))
        xp = xp + peer_ffn(rmsnorm(xp, norm_ffn[i]), peer_w_q[i], peer_sub_keys[i], peer_u[i], peer_v[i])
        xs = xs + peer_ffn(rmsnorm(xs, norm_ffn[i]), peer_w_q[i], peer_sub_keys[i], peer_u[i], peer_v[i])
    y_prompt = rmsnorm(xp, norm_final)[:, N_META:]
    y_sample = rmsnorm(xs, norm_final)
    return (y_prompt, y_sample,
            jnp.stack(new_k_p), jnp.stack(new_v_p), jnp.stack(new_sr_p), jnp.stack(new_si_p),
            jnp.stack(new_k_s), jnp.stack(new_v_s), jnp.stack(new_sr_s), jnp.stack(new_si_s))
```

```python
import functools
import math

import numpy as np
import jax
import jax.numpy as jnp
from jax import lax
from jax.experimental import pallas as pl
from jax.experimental.pallas import tpu as pltpu

F32 = jnp.float32
BF16 = jnp.bfloat16
EPS = 1e-6
PEER_TOPK = 16
LANES = 128
SUBLANES = 8
NEG_BIG = -1e30
VMEM_LIMIT = 56 * 1024 * 1024
NT_DIMS = (((1,), (1,)), ((), ()))


def _cparams(n_axes):
    return pltpu.CompilerParams(dimension_semantics=("arbitrary",) * n_axes,
                                vmem_limit_bytes=VMEM_LIMIT)


def _round_up(x, m):
    return (x + m - 1) // m * m


def _pick_tile(n, candidates):
    for c in candidates:
        if n % c == 0:
            return c
    return n


def _gelu(x):
    return 0.5 * x * (1.0 + lax.erf(x * (1.0 / math.sqrt(2.0))))


def _rms_scale(x):
    return lax.rsqrt(jnp.mean(x * x, axis=-1, keepdims=True) + EPS)


def _lambda_init(layer):
    return 0.8 - 0.6 * math.exp(-0.3 * layer)


def _norm_mm_kernel(x_ref, g_ref, w_ref, o_ref, h_ref):
    @pl.when(pl.program_id(1) == 0)
    def _():
        x = x_ref[...]
        h_ref[...] = (x * _rms_scale(x) * g_ref[...]).astype(BF16)

    o_ref[...] = jnp.dot(h_ref[...], w_ref[...], preferred_element_type=F32)


def _norm_matmul(x, g, w):
    n, d = x.shape
    m = w.shape[1]
    tm = _pick_tile(n, (640, 512, 256, 128))
    tn = _pick_tile(m, (1024, 512, 256, 128))
    return pl.pallas_call(
        _norm_mm_kernel,
        grid=(n // tm, m // tn),
        in_specs=[pl.BlockSpec((tm, d), lambda i, j: (i, 0)),
                  pl.BlockSpec((1, d), lambda i, j: (0, 0)),
                  pl.BlockSpec((d, tn), lambda i, j: (0, j))],
        out_specs=pl.BlockSpec((tm, tn), lambda i, j: (i, j)),
        out_shape=jax.ShapeDtypeStruct((n, m), F32),
        scratch_shapes=[pltpu.VMEM((tm, d), BF16)],
        compiler_params=_cparams(2),
        name="norm_matmul",
    )(x, g.reshape(1, d), w)


def _mm_res_kernel(a_ref, w_ref, r_ref, o_ref):
    o_ref[...] = r_ref[...] + jnp.dot(a_ref[...], w_ref[...], preferred_element_type=F32)


def _matmul_residual(a, w, res):
    n, k = a.shape
    m = w.shape[1]
    tm = _pick_tile(n, (640, 512, 256, 128))
    tn = _pick_tile(m, (1024, 512, 256, 128))
    return pl.pallas_call(
        _mm_res_kernel,
        grid=(n // tm, m // tn),
        in_specs=[pl.BlockSpec((tm, k), lambda i, j: (i, 0)),
                  pl.BlockSpec((k, tn), lambda i, j: (0, j)),
                  pl.BlockSpec((tm, tn), lambda i, j: (i, j))],
        out_specs=pl.BlockSpec((tm, tn), lambda i, j: (i, j)),
        out_shape=jax.ShapeDtypeStruct((n, m), F32),
        compiler_params=_cparams(2),
        name="matmul_residual",
    )(a, w, res)


def _glu_res_kernel(a_ref, wv_ref, wg_ref, r_ref, o_ref):
    a = a_ref[...]
    zv = jnp.dot(a, wv_ref[...], preferred_element_type=F32)
    zg = jnp.dot(a, wg_ref[...], preferred_element_type=F32)
    o_ref[...] = r_ref[...] + zv * jax.nn.sigmoid(zg)


def _glu_residual(a, w, res):
    n, k = a.shape
    m = w.shape[1] // 2
    tm = _pick_tile(n, (640, 512, 256, 128))
    tn = _pick_tile(m, (512, 256, 128))
    nj = m // tn
    return pl.pallas_call(
        _glu_res_kernel,
        grid=(n // tm, nj),
        in_specs=[pl.BlockSpec((tm, k), lambda i, j: (i, 0)),
                  pl.BlockSpec((k, tn), lambda i, j: (0, j)),
                  pl.BlockSpec((k, tn), lambda i, j: (0, j + nj)),
                  pl.BlockSpec((tm, tn), lambda i, j: (i, j))],
        out_specs=pl.BlockSpec((tm, tn), lambda i, j: (i, j)),
        out_shape=jax.ShapeDtypeStruct((n, m), F32),
        compiler_params=_cparams(2),
        name="glu_residual",
    )(a, w, w, res)


def _final_norm_kernel(x_ref, g_ref, o_ref):
    x = x_ref[...]
    o_ref[...] = x * _rms_scale(x) * g_ref[...]


def _final_norm(x, g):
    n, d = x.shape
    tm = _pick_tile(n, (640, 512, 256, 128))
    return pl.pallas_call(
        _final_norm_kernel,
        grid=(n // tm,),
        in_specs=[pl.BlockSpec((tm, d), lambda i: (i, 0)),
                  pl.BlockSpec((1, d), lambda i: (0, 0))],
        out_specs=pl.BlockSpec((tm, d), lambda i: (i, 0)),
        out_shape=jax.ShapeDtypeStruct((n, d), F32),
        compiler_params=_cparams(1),
        name="final_norm",
    )(x, g.reshape(1, d))


def _diff_lambda(lam_ref, lam_init):
    lv = lam_ref[...]
    s1 = jnp.sum(lv[0:1] * lv[1:2], axis=1, keepdims=True)
    s2 = jnp.sum(lv[2:3] * lv[3:4], axis=1, keepdims=True)
    return jnp.exp(s1) - jnp.exp(s2) + lam_init


def _softmax_update(sc, v, m_ref, l_ref, a_ref):
    m_old = m_ref[...]
    m_new = jnp.maximum(m_old, jnp.max(sc, axis=1, keepdims=True))
    alpha = jnp.exp(m_old - m_new)
    p = jnp.exp(sc - m_new)
    l_ref[...] = alpha * l_ref[...] + jnp.sum(p, axis=1, keepdims=True)
    a_ref[...] = alpha * a_ref[...] + jnp.dot(p.astype(BF16), v, preferred_element_type=F32)
    m_ref[...] = m_new


def _diff_finalize(lam_ref, g_ref, o_ref, l1, a1, l2, a2, lam_init):
    lam = _diff_lambda(lam_ref, lam_init)
    o = a1[...] / l1[...] - lam * (a2[...] / l2[...])
    o_ref[...] = ((o * _rms_scale(o) * g_ref[...]) * (1.0 - lam_init)).astype(o_ref.dtype)


def _init_softmax_state(m1, l1, a1, m2, l2, a2):
    for m, l, a in ((m1, l1, a1), (m2, l2, a2)):
        m[...] = jnp.full(m.shape, NEG_BIG, F32)
        l[...] = jnp.zeros(l.shape, F32)
        a[...] = jnp.zeros(a.shape, F32)


def _flash_prompt_kernel(qi_ref, kj_ref, slope_ref, q_ref, k_ref, v_ref, lam_ref, g_ref, o_ref,
                         qs, m1, l1, a1, m2, l2, a2, *, tq, hd, lam_init):
    h = pl.program_id(0)
    s = pl.program_id(1)
    qi = qi_ref[s]
    kj = kj_ref[s]

    @pl.when(kj == 0)
    def _():
        qs[...] = (q_ref[...] * (hd ** -0.5)).astype(BF16)
        _init_softmax_state(m1, l1, a1, m2, l2, a2)

    def update(masked):
        k = k_ref[...].astype(BF16)
        v = v_ref[...].astype(BF16)
        col = lax.broadcasted_iota(jnp.int32, (1, tq), 1)
        bias = slope_ref[h] * ((kj - qi) * tq + col).astype(F32)
        if masked:
            keep = (lax.broadcasted_iota(jnp.int32, (tq, tq), 1)
                    <= lax.broadcasted_iota(jnp.int32, (tq, tq), 0))
        for lo, m_ref, l_ref, a_ref in ((0, m1, l1, a1), (hd, m2, l2, a2)):
            sc = lax.dot_general(qs[:, lo:lo + hd], k[:, lo:lo + hd], NT_DIMS,
                                 preferred_element_type=F32) + bias
            if masked:
                sc = jnp.where(keep, sc, NEG_BIG)
            _softmax_update(sc, v, m_ref, l_ref, a_ref)

    @pl.when(kj < qi)
    def _():
        update(False)

    @pl.when(kj == qi)
    def _():
        update(True)
        _diff_finalize(lam_ref, g_ref, o_ref, l1, a1, l2, a2, lam_init)


def _attn_prompt(qkv, lam_vecs, subln, slopes, n_heads, hd, lam_init, tq):
    n = qkv.shape[0]
    nb = n // tq
    hw = 2 * hd
    pairs = [(i, j) for i in range(nb) for j in range(i + 1)]
    qi = jnp.asarray([p[0] for p in pairs], jnp.int32)
    kj = jnp.asarray([p[1] for p in pairs], jnp.int32)
    kern = functools.partial(_flash_prompt_kernel, tq=tq, hd=hd, lam_init=lam_init)
    grid_spec = pltpu.PrefetchScalarGridSpec(
        num_scalar_prefetch=3,
        grid=(n_heads, len(pairs)),
        in_specs=[pl.BlockSpec((tq, hw), lambda h, s, qi, kj, sl: (qi[s], h)),
                  pl.BlockSpec((tq, hw), lambda h, s, qi, kj, sl: (kj[s], n_heads + h)),
                  pl.BlockSpec((tq, hw), lambda h, s, qi, kj, sl: (kj[s], 2 * n_heads + h)),
                  pl.BlockSpec((4, hd), lambda h, s, qi, kj, sl: (0, 0)),
                  pl.BlockSpec((1, hw), lambda h, s, qi, kj, sl: (0, 0))],
        out_specs=pl.BlockSpec((tq, hw), lambda h, s, qi, kj, sl: (qi[s], h)),
        scratch_shapes=[pltpu.VMEM((tq, hw), BF16),
                        pltpu.VMEM((tq, 1), F32), pltpu.VMEM((tq, 1), F32), pltpu.VMEM((tq, hw), F32),
                        pltpu.VMEM((tq, 1), F32), pltpu.VMEM((tq, 1), F32), pltpu.VMEM((tq, hw), F32)])
    return pl.pallas_call(
        kern, grid_spec=grid_spec,
        out_shape=jax.ShapeDtypeStruct((n, n_heads * hw), BF16),
        compiler_params=_cparams(2),
        name="attn_prompt",
    )(qi, kj, slopes, qkv, qkv, qkv, lam_vecs, subln.reshape(1, hw))


def _attn_sample_kernel(pt_ref, q_ref, kn_ref, vn_ref, slope_ref, lam_ref, g_ref, *rest,
                        pps, hd, n_heads, page, past_len, lam_init):
    page_refs = rest[:2 * pps]
    o_ref = rest[2 * pps]
    qs, m1, l1, a1, m2, l2, a2 = rest[2 * pps + 1:]
    p = pl.program_id(1)
    rows = q_ref.shape[0]

    @pl.when(p == 0)
    def _():
        qs[...] = (q_ref[...] * (hd ** -0.5)).astype(BF16)
        _init_softmax_state(m1, l1, a1, m2, l2, a2)

    row = lax.broadcasted_iota(jnp.int32, (rows, 1), 0)
    t_row = row // n_heads
    h_row = row % n_heads
    slope = slope_ref[...]

    def attend(kf, vf, first_pos, causal):
        cols = kf.shape[0]
        col = lax.broadcasted_iota(jnp.int32, (1, cols), 1)
        tok = col // n_heads
        valid = h_row == col % n_heads
        if causal:
            valid = jnp.logical_and(valid, tok <= t_row)
        bias = slope * (first_pos + tok - (past_len + t_row)).astype(F32)
        for lo, m_ref, l_ref, a_ref in ((0, m1, l1, a1), (hd, m2, l2, a2)):
            sc = lax.dot_general(qs[:, lo:lo + hd], kf[:, lo:lo + hd], NT_DIMS,
                                 preferred_element_type=F32) + bias
            sc = jnp.where(valid, sc, NEG_BIG)
            _softmax_update(sc, vf, m_ref, l_ref, a_ref)

    for i in range(pps):
        kf = page_refs[i][...].reshape(page * n_heads, 2 * hd).astype(BF16)
        vf = page_refs[pps + i][...].reshape(page * n_heads, 2 * hd).astype(BF16)
        attend(kf, vf, (p * pps + i) * page, False)

    @pl.when(p == pl.num_programs(1) - 1)
    def _():
        attend(kn_ref[...].astype(BF16), vn_ref[...].astype(BF16), past_len, True)
        _diff_finalize(lam_ref, g_ref, o_ref, l1, a1, l2, a2, lam_init)


def _attn_sample(q, k_new, v_new, cache_k, cache_v, layer, page_table, lam_vecs, subln, slopes,
                 n_heads, hd, lam_init):
    nb, rows, hw = q.shape
    page = cache_k.shape[2]
    n_pages = page_table.shape[1]
    pps = _pick_tile(n_pages, (4, 2, 1))
    kern = functools.partial(_attn_sample_kernel, pps=pps, hd=hd, n_heads=n_heads, page=page,
                             past_len=n_pages * page, lam_init=lam_init)
    slope_rows = jnp.tile(slopes, rows // n_heads).reshape(rows, 1)

    def page_spec(i):
        return pl.BlockSpec((None, None, page, n_heads, hw),
                            lambda b, p, pt: (layer, pt[b * n_pages + p * pps + i], 0, 0, 0))

    row_spec = pl.BlockSpec((None, rows, hw), lambda b, p, pt: (b, 0, 0))
    grid_spec = pltpu.PrefetchScalarGridSpec(
        num_scalar_prefetch=1,
        grid=(nb, n_pages // pps),
        in_specs=[row_spec, row_spec, row_spec,
                  pl.BlockSpec((rows, 1), lambda b, p, pt: (0, 0)),
                  pl.BlockSpec((4, hd), lambda b, p, pt: (0, 0)),
                  pl.BlockSpec((1, hw), lambda b, p, pt: (0, 0))]
                 + [page_spec(i) for i in range(pps)] * 2,
        out_specs=row_spec,
        scratch_shapes=[pltpu.VMEM((rows, hw), BF16),
                        pltpu.VMEM((rows, 1), F32), pltpu.VMEM((rows, 1), F32), pltpu.VMEM((rows, hw), F32),
                        pltpu.VMEM((rows, 1), F32), pltpu.VMEM((rows, 1), F32), pltpu.VMEM((rows, hw), F32)])
    return pl.pallas_call(
        kern, grid_spec=grid_spec,
        out_shape=jax.ShapeDtypeStruct((nb, rows, hw), BF16),
        compiler_params=_cparams(2),
        name="attn_sample",
    )(page_table.reshape(-1), q, k_new, v_new, slope_rows, lam_vecs, subln.reshape(1, hw),
      *([cache_k] * pps), *([cache_v] * pps))


def _s5_discretize(lam_re, lam_im, log_dt, b_re, b_im):
    dt = jnp.exp(log_dt)[:, None]
    mag = jnp.exp(lam_re * dt)
    a_re = mag * jnp.cos(lam_im * dt)
    a_im = mag * jnp.sin(lam_im * dt)
    den = lam_re * lam_re + lam_im * lam_im
    nr = a_re - 1.0
    coef_re = (nr * lam_re + a_im * lam_im) / den
    coef_im = (a_im * lam_re - nr * lam_im) / den
    bb_re = coef_re[..., None] * b_re - coef_im[..., None] * b_im
    bb_im = coef_re[..., None] * b_im + coef_im[..., None] * b_re
    return a_re, a_im, bb_re, bb_im


def _s5_weights(lam_re, lam_im, log_dt, b_re, b_im, c_re, c_im):
    g, p = lam_re.shape
    c = b_re.shape[2]
    gpt = LANES // c
    nt = g // gpt
    a_re, a_im, bb_re, bb_im = _s5_discretize(lam_re, lam_im, log_dt, b_re, b_im)
    eye = jnp.eye(gpt, dtype=F32)

    def in_proj(bb):
        t = bb.reshape(nt, gpt, p, c)
        return jnp.einsum('jgpc,gh->jgchp', t, eye).reshape(nt, gpt * c, gpt * p)

    def out_proj(cm):
        t = cm.reshape(nt, gpt, c, p)
        return jnp.einsum('jgcp,gh->jgphc', t, eye).reshape(nt, gpt * p, gpt * c)

    wb = jnp.concatenate([in_proj(bb_re), in_proj(bb_im)], axis=2).astype(BF16)
    cc = jnp.concatenate([out_proj(c_re), -out_proj(c_im)], axis=1).astype(BF16)
    return a_re.reshape(-1), a_im.reshape(-1), wb, cc


def _s5_prompt_kernel(u_ref, wb_ref, cc_ref, are_ref, aim_ref, d_ref, y_ref, sre_ref, sim_ref,
                      bre, bim, pre, pim, cre, cim, *, tt, seg, last_row, spt):
    i = pl.program_id(0)
    n_slabs = bre.shape[0]
    n_tiles = u_ref.shape[1] // LANES

    @pl.when(i == 0)
    def _():
        cre[...] = jnp.zeros(cre.shape, F32)
        cim[...] = jnp.zeros(cim.shape, F32)
        a_r = are_ref[...]
        a_i = aim_ref[...]
        p_r, p_i = a_r, a_i
        for k in range(seg):
            pre[:, k:k + 1, :] = p_r
            pim[:, k:k + 1, :] = p_i
            p_r, p_i = p_r * a_r - p_i * a_i, p_r * a_i + p_i * a_r

    u = u_ref[...]
    for j in range(n_tiles):
        r = jnp.dot(u[:, j * LANES:(j + 1) * LANES].astype(BF16), wb_ref[j], preferred_element_type=F32)
        for q in range(spt):
            bre[j * spt + q] = r[:, q * LANES:(q + 1) * LANES]
            bim[j * spt + q] = r[:, (spt + q) * LANES:(spt + q + 1) * LANES]

    def slab_body(s, carry):
        a_r = jnp.broadcast_to(are_ref[s], (SUBLANES, LANES))
        a_i = jnp.broadcast_to(aim_ref[s], (SUBLANES, LANES))
        x_r = jnp.zeros((SUBLANES, LANES), F32)
        x_i = jnp.zeros((SUBLANES, LANES), F32)
        local = []
        for k in range(seg):
            b_r = bre[s, pl.ds(k, SUBLANES, stride=seg), :]
            b_i = bim[s, pl.ds(k, SUBLANES, stride=seg), :]
            x_r, x_i = a_r * x_r - a_i * x_i + b_r, a_r * x_i + a_i * x_r + b_i
            local.append((x_r, x_i))
        as_r = pre[s, seg - 1:seg, :]
        as_i = pim[s, seg - 1:seg, :]
        s_r = cre[s]
        s_i = cim[s]
        st_r, st_i = [], []
        for g in range(SUBLANES):
            st_r.append(s_r)
            st_i.append(s_i)
            e_r = x_r[g:g + 1]
            e_i = x_i[g:g + 1]
            s_r, s_i = as_r * s_r - as_i * s_i + e_r, as_r * s_i + as_i * s_r + e_i
        cre[s] = s_r
        cim[s] = s_i
        st_r = jnp.concatenate(st_r, axis=0)
        st_i = jnp.concatenate(st_i, axis=0)
        for k in range(seg):
            p_r = jnp.broadcast_to(pre[s, k:k + 1, :], (SUBLANES, LANES))
            p_i = jnp.broadcast_to(pim[s, k:k + 1, :], (SUBLANES, LANES))
            l_r, l_i = local[k]
            bre[s, pl.ds(k, SUBLANES, stride=seg), :] = l_r + p_r * st_r - p_i * st_i
            bim[s, pl.ds(k, SUBLANES, stride=seg), :] = l_i + p_r * st_i + p_i * st_r
        return carry

    lax.fori_loop(0, n_slabs, slab_body, 0)

    for j in range(n_tiles):
        xcat = jnp.concatenate([bre[j * spt + q] for q in range(spt)]
                               + [bim[j * spt + q] for q in range(spt)], axis=1).astype(BF16)
        sl = slice(j * LANES, (j + 1) * LANES)
        y = jnp.dot(xcat, cc_ref[j], preferred_element_type=F32) + d_ref[:, sl] * u[:, sl]
        y_ref[:, sl] = _gelu(y).astype(y_ref.dtype)

    @pl.when(i == pl.num_programs(0) - 1)
    def _():
        sre_ref[...] = bre[:, last_row:last_row + 1, :]
        sim_ref[...] = bim[:, last_row:last_row + 1, :]


def _s5_prompt(u, a_re, a_im, wb, cc, d, n_valid, tt):
    n, dm = u.shape
    nt, _, two_w = wb.shape
    spt = two_w // 2 // LANES
    n_slabs = nt * spt
    seg = tt // SUBLANES
    nblk = n // tt
    assert (n_valid - 1) // tt == nblk - 1
    kern = functools.partial(_s5_prompt_kernel, tt=tt, seg=seg, last_row=(n_valid - 1) % tt, spt=spt)
    slab = jax.ShapeDtypeStruct((n_slabs, 1, LANES), F32)
    const3 = lambda i: (0, 0, 0)
    y, sre, sim = pl.pallas_call(
        kern, grid=(nblk,),
        in_specs=[pl.BlockSpec((tt, dm), lambda i: (i, 0)),
                  pl.BlockSpec(wb.shape, const3),
                  pl.BlockSpec(cc.shape, const3),
                  pl.BlockSpec((n_slabs, 1, LANES), const3),
                  pl.BlockSpec((n_slabs, 1, LANES), const3),
                  pl.BlockSpec((1, dm), lambda i: (0, 0))],
        out_specs=[pl.BlockSpec((tt, dm), lambda i: (i, 0)),
                   pl.BlockSpec((n_slabs, 1, LANES), const3),
                   pl.BlockSpec((n_slabs, 1, LANES), const3)],
        out_shape=[jax.ShapeDtypeStruct((n, dm), BF16), slab, slab],
        scratch_shapes=[pltpu.VMEM((n_slabs, tt, LANES), F32), pltpu.VMEM((n_slabs, tt, LANES), F32),
                        pltpu.VMEM((n_slabs, seg, LANES), F32), pltpu.VMEM((n_slabs, seg, LANES), F32),
                        pltpu.VMEM((n_slabs, 1, LANES), F32), pltpu.VMEM((n_slabs, 1, LANES), F32)],
        compiler_params=_cparams(1),
        name="s5_prompt",
    )(u, wb, cc, a_re.reshape(n_slabs, 1, LANES), a_im.reshape(n_slabs, 1, LANES), d.reshape(1, dm))
    return y, sre.reshape(-1), sim.reshape(-1)


def _s5_sample_kernel(u_ref, wb_ref, cc_ref, are_ref, aim_ref, d_ref, s0r_ref, s0i_ref,
                      y_ref, sre_ref, sim_ref, bre, bim):
    n_steps = u_ref.shape[0]
    n_tiles = u_ref.shape[2] // LANES
    half = wb_ref.shape[2] // 2
    a_r = are_ref[...]
    a_i = aim_ref[...]
    x_r = s0r_ref[...]
    x_i = s0i_ref[...]
    for t in range(n_steps):
        u = u_ref[t]
        for j in range(n_tiles):
            r = jnp.dot(u[:, j * LANES:(j + 1) * LANES].astype(BF16), wb_ref[j], preferred_element_type=F32)
            bre[:, j * half:(j + 1) * half] = r[:, :half]
            bim[:, j * half:(j + 1) * half] = r[:, half:]
        x_r, x_i = a_r * x_r - a_i * x_i + bre[...], a_r * x_i + a_i * x_r + bim[...]
        for j in range(n_tiles):
            xcat = jnp.concatenate([x_r[:, j * half:(j + 1) * half], x_i[:, j * half:(j + 1) * half]],
                                   axis=1).astype(BF16)
            sl = slice(j * LANES, (j + 1) * LANES)
            y = jnp.dot(xcat, cc_ref[j], preferred_element_type=F32) + d_ref[:, sl] * u[:, sl]
            y_ref[t, :, sl] = _gelu(y).astype(y_ref.dtype)
    sre_ref[...] = x_r
    sim_ref[...] = x_i


def _s5_sample(u, a_re, a_im, wb, cc, d, s0_re, s0_im):
    n_steps, nb, dm = u.shape
    width = s0_re.shape[1]
    tb = _pick_tile(nb, (32, 16, 8))
    const3 = lambda i: (0, 0, 0)
    st = jax.ShapeDtypeStruct((nb, width), F32)
    return pl.pallas_call(
        _s5_sample_kernel, grid=(nb // tb,),
        in_specs=[pl.BlockSpec((n_steps, tb, dm), lambda i: (0, i, 0)),
                  pl.BlockSpec(wb.shape, const3),
                  pl.BlockSpec(cc.shape, const3),
                  pl.BlockSpec((1, width), lambda i: (0, 0)),
                  pl.BlockSpec((1, width), lambda i: (0, 0)),
                  pl.BlockSpec((1, dm), lambda i: (0, 0)),
                  pl.BlockSpec((tb, width), lambda i: (i, 0)),
                  pl.BlockSpec((tb, width), lambda i: (i, 0))],
        out_specs=[pl.BlockSpec((n_steps, tb, dm), lambda i: (0, i, 0)),
                   pl.BlockSpec((tb, width), lambda i: (i, 0)),
                   pl.BlockSpec((tb, width), lambda i: (i, 0))],
        out_shape=[jax.ShapeDtypeStruct((n_steps, nb, dm), BF16), st, st],
        scratch_shapes=[pltpu.VMEM((tb, width), F32), pltpu.VMEM((tb, width), F32)],
        compiler_params=_cparams(1),
        name="s5_sample",
    )(u, wb, cc, a_re.reshape(1, width), a_im.reshape(1, width), d.reshape(1, dm), s0_re, s0_im)


def _topk_rows(s, k):
    n = s.shape[0]
    row = lax.broadcasted_iota(jnp.int32, s.shape, 0)
    vals, idxs = [], []
    for _ in range(k):
        m = jnp.max(s, axis=0, keepdims=True)
        idx = jnp.min(jnp.where(s == m, row, n), axis=0, keepdims=True)
        vals.append(m)
        idxs.append(idx)
        s = jnp.where(row == idx, -jnp.inf, s)
    return jnp.concatenate(vals, axis=0), jnp.concatenate(idxs, axis=0)


def _candidate_pairs(k):
    return [(i, j) for i in range(k) for j in range(k) if (i + 1) * (j + 1) <= k]


def _peer_router_kernel(x_ref, g_ref, wq_ref, keys_ref, h_ref, gt_ref, i1_ref, i2_ref, q_s,
                        *, n_heads, topk):
    x = x_ref[...]
    hb = (x * _rms_scale(x) * g_ref[...]).astype(BF16)
    h_ref[...] = hb
    q_s[...] = jnp.dot(hb, wq_ref[...], preferred_element_type=F32).astype(BF16)
    tm = x.shape[0]
    half = keys_ref.shape[2]
    n_chunks = tm // LANES
    pairs = _candidate_pairs(topk)
    counts = [sum(1 for p in pairs if p[0] == i) for i in range(topk)]
    n_cand = _round_up(len(pairs), SUBLANES)

    def body(it, carry):
        hd = it % n_heads
        r0 = pl.multiple_of((it // n_heads) * LANES, LANES)
        q = q_s[pl.ds(r0, LANES), pl.ds(pl.multiple_of(hd * 2 * half, 2 * half), 2 * half)]
        s0 = lax.dot_general(keys_ref[0], q[:, :half], NT_DIMS, preferred_element_type=F32)
        s1 = lax.dot_general(keys_ref[1], q[:, half:], NT_DIMS, preferred_element_type=F32)
        sv0, si0 = _topk_rows(s0, topk)
        sv1, si1 = _topk_rows(s1, topk)
        si0 = si0.astype(F32)
        si1 = si1.astype(F32)
        cs, c1, c2 = [], [], []
        for i in range(topk):
            cnt = counts[i]
            cs.append(sv0[i:i + 1] + sv1[0:cnt])
            c1.append(jnp.broadcast_to(si0[i:i + 1], (cnt, LANES)))
            c2.append(si1[0:cnt])
        pad = n_cand - len(pairs)
        if pad:
            cs.append(jnp.full((pad, LANES), -jnp.inf, F32))
            c1.append(jnp.zeros((pad, LANES), F32))
            c2.append(jnp.zeros((pad, LANES), F32))
        cand = jnp.concatenate(cs, axis=0)
        cand1 = jnp.concatenate(c1, axis=0)
        cand2 = jnp.concatenate(c2, axis=0)
        row = lax.broadcasted_iota(jnp.int32, cand.shape, 0)
        ts, t1, t2 = [], [], []
        for _ in range(topk):
            m = jnp.max(cand, axis=0, keepdims=True)
            idx = jnp.min(jnp.where(cand == m, row, n_cand), axis=0, keepdims=True)
            sel = row == idx
            ts.append(m)
            t1.append(jnp.sum(jnp.where(sel, cand1, 0.0), axis=0, keepdims=True))
            t2.append(jnp.sum(jnp.where(sel, cand2, 0.0), axis=0, keepdims=True))
            cand = jnp.where(sel, -jnp.inf, cand)
        ts = jnp.concatenate(ts, axis=0)
        e = jnp.exp(ts - ts[0:1])
        gate = e / jnp.sum(e, axis=0, keepdims=True)
        rows = pl.ds(pl.multiple_of(hd * topk, topk), topk)
        cols = pl.ds(r0, LANES)
        gt_ref[rows, cols] = gate
        i1_ref[rows, cols] = jnp.concatenate(t1, axis=0)
        i2_ref[rows, cols] = jnp.concatenate(t2, axis=0)
        return carry

    lax.fori_loop(0, n_heads * n_chunks, body, 0)


def _peer_router(x, g, w_q, sub_keys):
    n, d = x.shape
    n_keys, half = sub_keys.shape[1], sub_keys.shape[2]
    assert n_keys == LANES and half == LANES
    n_heads = w_q.shape[1] // (2 * half)
    hk = n_heads * PEER_TOPK
    tm = _pick_tile(n, (640, 512, 256, 128))
    kern = functools.partial(_peer_router_kernel, n_heads=n_heads, topk=PEER_TOPK)
    tr = jax.ShapeDtypeStruct((hk, n), F32)
    return pl.pallas_call(
        kern, grid=(n // tm,),
        in_specs=[pl.BlockSpec((tm, d), lambda i: (i, 0)),
                  pl.BlockSpec((1, d), lambda i: (0, 0)),
                  pl.BlockSpec(w_q.shape, lambda i: (0, 0)),
                  pl.BlockSpec(sub_keys.shape, lambda i: (0, 0, 0))],
        out_specs=[pl.BlockSpec((tm, d), lambda i: (i, 0)),
                   pl.BlockSpec((hk, tm), lambda i: (0, i)),
                   pl.BlockSpec((hk, tm), lambda i: (0, i)),
                   pl.BlockSpec((hk, tm), lambda i: (0, i))],
        out_shape=[jax.ShapeDtypeStruct((n, d), BF16), tr, tr, tr],
        scratch_shapes=[pltpu.VMEM((tm, w_q.shape[1]), BF16)],
        compiler_params=_cparams(1),
        name="peer_router",
    )(x, g.reshape(1, d), w_q, sub_keys)


def _peer_expand_kernel(gt_ref, i1_ref, i2_ref, w_ref, g_s, a_s, b_s, scr, *, pitch):
    g_s[...] = gt_ref[...].T
    a_s[...] = i1_ref[...].T
    b_s[...] = i2_ref[...].T
    tn, hk = g_s.shape
    n_keys = w_ref.shape[1] // LANES
    sub = lax.broadcasted_iota(jnp.int32, (n_keys, hk), 0).astype(F32)

    def body(n, carry):
        at = jnp.where(a_s[pl.ds(n, 1), :] == sub, g_s[pl.ds(n, 1), :], 0.0).astype(BF16)
        bt = jnp.where(b_s[pl.ds(n, 1), :] == sub, 1.0, 0.0).astype(BF16)
        scr[pl.ds(pl.multiple_of(n * pitch, SUBLANES), n_keys), :] = lax.dot_general(
            at, bt, NT_DIMS, preferred_element_type=F32)
        return carry

    lax.fori_loop(0, tn, body, 0)
    for a in range(n_keys):
        w_ref[:, a * LANES:(a + 1) * LANES] = scr[pl.ds(a, tn, stride=pitch), :].astype(w_ref.dtype)


def _peer_expand(gt, i1, i2, n_keys):
    hk, n = gt.shape
    assert n_keys == LANES and hk == LANES
    tn = LANES
    pitch = n_keys + SUBLANES
    kern = functools.partial(_peer_expand_kernel, pitch=pitch)
    spec = pl.BlockSpec((hk, tn), lambda i: (0, i))
    return pl.pallas_call(
        kern, grid=(n // tn,),
        in_specs=[spec, spec, spec],
        out_specs=pl.BlockSpec((tn, n_keys * n_keys), lambda i: (i, 0)),
        out_shape=jax.ShapeDtypeStruct((n, n_keys * n_keys), BF16),
        scratch_shapes=[pltpu.VMEM((tn, hk), F32), pltpu.VMEM((tn, hk), F32), pltpu.VMEM((tn, hk), F32),
                        pltpu.VMEM((tn * pitch, LANES), F32)],
        compiler_params=_cparams(1),
        name="peer_expand",
    )(gt, i1, i2)


def _peer_dense_kernel(h_ref, u_ref, v_ref, w_ref, r_ref, o_ref):
    @pl.when(pl.program_id(1) == 0)
    def _():
        o_ref[...] = r_ref[...]

    act = lax.dot_general(h_ref[...], u_ref[...], NT_DIMS, preferred_element_type=F32)
    p = (w_ref[...].astype(F32) * _gelu(act)).astype(BF16)
    o_ref[...] += jnp.dot(p, v_ref[...], preferred_element_type=F32)


def _peer_dense(h, u_tab, v_tab, w, res):
    n, d = h.shape
    n_exp = u_tab.shape[0]
    tm = _pick_tile(n, (640, 512, 256, 128))
    te = 512
    return pl.pallas_call(
        _peer_dense_kernel, grid=(n // tm, n_exp // te),
        in_specs=[pl.BlockSpec((tm, d), lambda i, e: (i, 0)),
                  pl.BlockSpec((te, d), lambda i, e: (e, 0)),
                  pl.BlockSpec((te, d), lambda i, e: (e, 0)),
                  pl.BlockSpec((tm, te), lambda i, e: (i, e)),
                  pl.BlockSpec((tm, d), lambda i, e: (i, 0))],
        out_specs=pl.BlockSpec((tm, d), lambda i, e: (i, 0)),
        out_shape=jax.ShapeDtypeStruct((n, d), F32),
        compiler_params=_cparams(2),
        name="peer_dense",
    )(h, u_tab, v_tab, w, res)


def _peer_ffn(x, g, w_q, sub_keys, u_tab, v_tab):
    h, gt, i1, i2 = _peer_router(x, g, w_q, sub_keys)
    w = _peer_expand(gt, i1, i2, sub_keys.shape[1])
    return _peer_dense(h, u_tab, v_tab, w, x)


def kernel(x_prompt, x_sample, cache_k, cache_v, state_ssm_re, state_ssm_im, page_table, meta_tokens, norm_mix, norm_ffn, norm_final, attn_w_qkv, attn_lambda, attn_subln, attn_w_o, ssm_w_in, ssm_lambda_re, ssm_lambda_im, ssm_log_dt, ssm_b_re, ssm_b_im, ssm_c_re, ssm_c_im, ssm_d, ssm_w_glu, peer_w_q, peer_sub_keys, peer_u, peer_v):
    batch, seq, d_model = x_prompt.shape
    assert batch == 1
    dec_batch, dec_seq, _ = x_sample.shape
    n_meta = meta_tokens.shape[0]
    depth = norm_mix.shape[0]
    n_heads, hw = cache_k.shape[3], cache_k.shape[4]
    hd = hw // 2
    n_groups, state_dim = ssm_lambda_re.shape[1], ssm_lambda_re.shape[2]
    t_tot = n_meta + seq
    tq = 640
    tt = 160
    n_p = _round_up(t_tot, math.lcm(tq, tt))
    n_s = dec_batch * dec_seq
    slopes = jnp.asarray(2.0 ** (-8.0 * np.arange(1, n_heads + 1) / n_heads), dtype=F32)

    xp = jnp.concatenate([meta_tokens.astype(F32), x_prompt[0],
                          jnp.zeros((n_p - t_tot, d_model), F32)], axis=0)
    xs = x_sample.reshape(n_s, d_model)

    outs = {k: [] for k in ("kp", "vp", "ks", "vs", "srp", "sip", "srs", "sis")}
    for i in range(depth):
        j = i // 2
        if i % 2 == 0:
            lam_init = _lambda_init(i)
            w_qkv = attn_w_qkv[j].astype(BF16)
            w_o = attn_w_o[j].astype(BF16)
            qk_w = n_heads * hw
            qkv_p = _norm_matmul(xp, norm_mix[i], w_qkv)
            qkv_s = _norm_matmul(xs, norm_mix[i], w_qkv)
            o_p = _attn_prompt(qkv_p, attn_lambda[j], attn_subln[j], slopes, n_heads, hd, lam_init, tq)

            def rows(a):
                return a.reshape(dec_batch, dec_seq * n_heads, hw)

            o_s = _attn_sample(rows(qkv_s[:, :qk_w]), rows(qkv_s[:, qk_w:2 * qk_w]), rows(qkv_s[:, 2 * qk_w:]),
                               cache_k, cache_v, j, page_table, attn_lambda[j], attn_subln[j], slopes,
                               n_heads, hd, lam_init)
            xp = _matmul_residual(o_p, w_o, xp)
            xs = _matmul_residual(o_s.reshape(n_s, qk_w), w_o, xs)
            outs["kp"].append(qkv_p[:t_tot, qk_w:2 * qk_w].reshape(batch, t_tot, n_heads, hw))
            outs["vp"].append(qkv_p[:t_tot, 2 * qk_w:].reshape(batch, t_tot, n_heads, hw))
            outs["ks"].append(qkv_s[:, qk_w:2 * qk_w].reshape(dec_batch, dec_seq, n_heads, hw))
            outs["vs"].append(qkv_s[:, 2 * qk_w:].reshape(dec_batch, dec_seq, n_heads, hw))
        else:
            w_in = ssm_w_in[j].astype(BF16)
            w_glu = ssm_w_glu[j].astype(BF16)
            a_re, a_im, wb, cc = _s5_weights(ssm_lambda_re[j], ssm_lambda_im[j], ssm_log_dt[j],
                                             ssm_b_re[j], ssm_b_im[j], ssm_c_re[j], ssm_c_im[j])
            d_vec = ssm_d[j].reshape(-1)
            u_p = _norm_matmul(xp, norm_mix[i], w_in)
            y_p, sr_p, si_p = _s5_prompt(u_p, a_re, a_im, wb, cc, d_vec, t_tot, tt)
            xp = _glu_residual(y_p, w_glu, xp)
            u_s = _norm_matmul(xs, norm_mix[i], w_in)
            u_s = u_s.reshape(dec_batch, dec_seq, d_model).transpose(1, 0, 2)
            y_s, sr_s, si_s = _s5_sample(u_s, a_re, a_im, wb, cc, d_vec,
                                         state_ssm_re[j].reshape(dec_batch, -1),
                                         state_ssm_im[j].reshape(dec_batch, -1))
            xs = _glu_residual(y_s.transpose(1, 0, 2).reshape(n_s, d_model), w_glu, xs)
            outs["srp"].append(sr_p.reshape(batch, n_groups, state_dim))
            outs["sip"].append(si_p.reshape(batch, n_groups, state_dim))
            outs["srs"].append(sr_s.reshape(dec_batch, n_groups, state_dim))
            outs["sis"].append(si_s.reshape(dec_batch, n_groups, state_dim))
        w_q = peer_w_q[i].astype(BF16)
        keys = peer_sub_keys[i].astype(BF16)
        u_tab = peer_u[i].astype(BF16)
        v_tab = peer_v[i].astype(BF16)
        xp = _peer_ffn(xp, norm_ffn[i], w_q, keys, u_tab, v_tab)
        xs = _peer_ffn(xs, norm_ffn[i], w_q, keys, u_tab, v_tab)

    y_prompt = _final_norm(xp, norm_final)[n_meta:t_tot].reshape(batch, seq, d_model)
    y_sample = _final_norm(xs, norm_final).reshape(dec_batch, dec_seq, d_model)
    return (y_prompt, y_sample,
            jnp.stack(outs["kp"]), jnp.stack(outs["vp"]), jnp.stack(outs["srp"]), jnp.stack(outs["sip"]),
            jnp.stack(outs["ks"]), jnp.stack(outs["vs"]), jnp.stack(outs["srs"]), jnp.stack(outs["sis"]))
```

```python
import functools
import math

import numpy as np
import jax
import jax.numpy as jnp
from jax import lax
from jax.experimental import pallas as pl
from jax.experimental.pallas import tpu as pltpu

F32 = jnp.float32
BF16 = jnp.bfloat16
EPS = 1e-6
PEER_TOPK = 16
LANES = 128
SUBLANES = 8
NEG_BIG = -1e30
LOG2E = math.log2(math.e)
VMEM_LIMIT = 56 * 1024 * 1024
NT_DIMS = (((1,), (1,)), ((), ()))


def _cparams(n_axes):
    return pltpu.CompilerParams(dimension_semantics=("arbitrary",) * n_axes,
                                vmem_limit_bytes=VMEM_LIMIT)


def _round_up(x, m):
    return (x + m - 1) // m * m


def _pick_tile(n, candidates):
    for c in candidates:
        if n % c == 0:
            return c
    return n


def _gelu(x):
    return 0.5 * x * (1.0 + lax.erf(x * (1.0 / math.sqrt(2.0))))


def _rms_scale(x):
    return lax.rsqrt(jnp.mean(x * x, axis=-1, keepdims=True) + EPS)


def _lambda_init(layer):
    return 0.8 - 0.6 * math.exp(-0.3 * layer)


def _norm_mm_kernel(x_ref, g_ref, w_ref, o_ref, h_ref):
    @pl.when(pl.program_id(1) == 0)
    def _():
        x = x_ref[...]
        h_ref[...] = (x * _rms_scale(x) * g_ref[...]).astype(BF16)

    o_ref[...] = jnp.dot(h_ref[...], w_ref[...], preferred_element_type=F32)


def _norm_matmul(x, g, w):
    n, d = x.shape
    m = w.shape[1]
    tm = _pick_tile(n, (640, 512, 256, 128))
    tn = _pick_tile(m, (1024, 512, 256, 128))
    return pl.pallas_call(
        _norm_mm_kernel,
        grid=(n // tm, m // tn),
        in_specs=[pl.BlockSpec((tm, d), lambda i, j: (i, 0)),
                  pl.BlockSpec((1, d), lambda i, j: (0, 0)),
                  pl.BlockSpec((d, tn), lambda i, j: (0, j))],
        out_specs=pl.BlockSpec((tm, tn), lambda i, j: (i, j)),
        out_shape=jax.ShapeDtypeStruct((n, m), F32),
        scratch_shapes=[pltpu.VMEM((tm, d), BF16)],
        compiler_params=_cparams(2),
        name="norm_matmul",
    )(x, g.reshape(1, d), w)


def _mm_res_kernel(a_ref, w_ref, r_ref, o_ref):
    o_ref[...] = r_ref[...] + jnp.dot(a_ref[...], w_ref[...], preferred_element_type=F32)


def _matmul_residual(a, w, res):
    n, k = a.shape
    m = w.shape[1]
    tm = _pick_tile(n, (640, 512, 256, 128))
    tn = _pick_tile(m, (1024, 512, 256, 128))
    return pl.pallas_call(
        _mm_res_kernel,
        grid=(n // tm, m // tn),
        in_specs=[pl.BlockSpec((tm, k), lambda i, j: (i, 0)),
                  pl.BlockSpec((k, tn), lambda i, j: (0, j)),
                  pl.BlockSpec((tm, tn), lambda i, j: (i, j))],
        out_specs=pl.BlockSpec((tm, tn), lambda i, j: (i, j)),
        out_shape=jax.ShapeDtypeStruct((n, m), F32),
        compiler_params=_cparams(2),
        name="matmul_residual",
    )(a, w, res)


def _glu_res_kernel(a_ref, wv_ref, wg_ref, r_ref, o_ref):
    a = a_ref[...]
    zv = jnp.dot(a, wv_ref[...], preferred_element_type=F32)
    zg = jnp.dot(a, wg_ref[...], preferred_element_type=F32)
    o_ref[...] = r_ref[...] + zv * jax.nn.sigmoid(zg)


def _glu_residual(a, w, res):
    n, k = a.shape
    m = w.shape[1] // 2
    tm = _pick_tile(n, (640, 512, 256, 128))
    tn = _pick_tile(m, (512, 256, 128))
    nj = m // tn
    return pl.pallas_call(
        _glu_res_kernel,
        grid=(n // tm, nj),
        in_specs=[pl.BlockSpec((tm, k), lambda i, j: (i, 0)),
                  pl.BlockSpec((k, tn), lambda i, j: (0, j)),
                  pl.BlockSpec((k, tn), lambda i, j: (0, j + nj)),
                  pl.BlockSpec((tm, tn), lambda i, j: (i, j))],
        out_specs=pl.BlockSpec((tm, tn), lambda i, j: (i, j)),
        out_shape=jax.ShapeDtypeStruct((n, m), F32),
        compiler_params=_cparams(2),
        name="glu_residual",
    )(a, w, w, res)


def _final_norm_kernel(x_ref, g_ref, o_ref):
    x = x_ref[...]
    o_ref[...] = x * _rms_scale(x) * g_ref[...]


def _final_norm(x, g):
    n, d = x.shape
    tm = _pick_tile(n, (640, 512, 256, 128))
    return pl.pallas_call(
        _final_norm_kernel,
        grid=(n // tm,),
        in_specs=[pl.BlockSpec((tm, d), lambda i: (i, 0)),
                  pl.BlockSpec((1, d), lambda i: (0, 0))],
        out_specs=pl.BlockSpec((tm, d), lambda i: (i, 0)),
        out_shape=jax.ShapeDtypeStruct((n, d), F32),
        compiler_params=_cparams(1),
        name="final_norm",
    )(x, g.reshape(1, d))


def _diff_lambda(lam_ref, lam_init):
    lv = lam_ref[...]
    s1 = jnp.sum(lv[0:1] * lv[1:2], axis=1, keepdims=True)
    s2 = jnp.sum(lv[2:3] * lv[3:4], axis=1, keepdims=True)
    return jnp.exp(s1) - jnp.exp(s2) + lam_init


def _softmax_update(sc, v, m_ref, l_ref, a_ref):
    m_old = m_ref[...]
    m_new = jnp.maximum(m_old, jnp.max(sc, axis=1, keepdims=True))
    alpha = jnp.exp2(m_old - m_new)
    p = jnp.exp2(sc - m_new)
    l_ref[...] = alpha * l_ref[...] + jnp.sum(p, axis=1, keepdims=True)
    a_ref[...] = alpha * a_ref[...] + jnp.dot(p.astype(BF16), v, preferred_element_type=F32)
    m_ref[...] = m_new


def _diff_finalize(lam_ref, g_ref, o_ref, l1, a1, l2, a2, lam_init):
    lam = _diff_lambda(lam_ref, lam_init)
    o = a1[...] / l1[...] - lam * (a2[...] / l2[...])
    o_ref[...] = ((o * _rms_scale(o) * g_ref[...]) * (1.0 - lam_init)).astype(o_ref.dtype)


def _init_softmax_state(m1, l1, a1, m2, l2, a2):
    for m, l, a in ((m1, l1, a1), (m2, l2, a2)):
        m[...] = jnp.full(m.shape, NEG_BIG, F32)
        l[...] = jnp.zeros(l.shape, F32)
        a[...] = jnp.zeros(a.shape, F32)


def _flash_prompt_kernel(qi_ref, kj_ref, slope_ref, q_ref, k_ref, v_ref, lam_ref, g_ref, o_ref,
                         qs, m1, l1, a1, m2, l2, a2, *, tq, hd, lam_init):
    h = pl.program_id(0)
    s = pl.program_id(1)
    qi = qi_ref[s]
    kj = kj_ref[s]

    @pl.when(kj == 0)
    def _():
        qs[...] = (q_ref[...] * (hd ** -0.5 * LOG2E)).astype(BF16)
        _init_softmax_state(m1, l1, a1, m2, l2, a2)

    def update(masked):
        k = k_ref[...].astype(BF16)
        v = v_ref[...].astype(BF16)
        col = lax.broadcasted_iota(jnp.int32, (1, tq), 1)
        bias = slope_ref[h] * ((kj - qi) * tq + col).astype(F32)
        if masked:
            keep = (lax.broadcasted_iota(jnp.int32, (tq, tq), 1)
                    <= lax.broadcasted_iota(jnp.int32, (tq, tq), 0))
        for lo, m_ref, l_ref, a_ref in ((0, m1, l1, a1), (hd, m2, l2, a2)):
            sc = lax.dot_general(qs[:, lo:lo + hd], k[:, lo:lo + hd], NT_DIMS,
                                 preferred_element_type=F32) + bias
            if masked:
                sc = jnp.where(keep, sc, NEG_BIG)
            _softmax_update(sc, v, m_ref, l_ref, a_ref)

    @pl.when(kj < qi)
    def _():
        update(False)

    @pl.when(kj == qi)
    def _():
        update(True)
        _diff_finalize(lam_ref, g_ref, o_ref, l1, a1, l2, a2, lam_init)


def _attn_prompt(qkv, lam_vecs, subln, slopes, n_heads, hd, lam_init, tq):
    n = qkv.shape[0]
    nb = n // tq
    hw = 2 * hd
    pairs = [(i, j) for i in range(nb) for j in range(i + 1)]
    qi = jnp.asarray([p[0] for p in pairs], jnp.int32)
    kj = jnp.asarray([p[1] for p in pairs], jnp.int32)
    kern = functools.partial(_flash_prompt_kernel, tq=tq, hd=hd, lam_init=lam_init)
    grid_spec = pltpu.PrefetchScalarGridSpec(
        num_scalar_prefetch=3,
        grid=(n_heads, len(pairs)),
        in_specs=[pl.BlockSpec((tq, hw), lambda h, s, qi, kj, sl: (qi[s], h)),
                  pl.BlockSpec((tq, hw), lambda h, s, qi, kj, sl: (kj[s], n_heads + h)),
                  pl.BlockSpec((tq, hw), lambda h, s, qi, kj, sl: (kj[s], 2 * n_heads + h)),
                  pl.BlockSpec((4, hd), lambda h, s, qi, kj, sl: (0, 0)),
                  pl.BlockSpec((1, hw), lambda h, s, qi, kj, sl: (0, 0))],
        out_specs=pl.BlockSpec((tq, hw), lambda h, s, qi, kj, sl: (qi[s], h)),
        scratch_shapes=[pltpu.VMEM((tq, hw), BF16),
                        pltpu.VMEM((tq, 1), F32), pltpu.VMEM((tq, 1), F32), pltpu.VMEM((tq, hw), F32),
                        pltpu.VMEM((tq, 1), F32), pltpu.VMEM((tq, 1), F32), pltpu.VMEM((tq, hw), F32)])
    return pl.pallas_call(
        kern, grid_spec=grid_spec,
        out_shape=jax.ShapeDtypeStruct((n, n_heads * hw), BF16),
        compiler_params=_cparams(2),
        name="attn_prompt",
    )(qi, kj, slopes, qkv, qkv, qkv, lam_vecs, subln.reshape(1, hw))


def _attn_sample_kernel(pt_ref, q_ref, kn_ref, vn_ref, slope_ref, lam_ref, g_ref, *rest,
                        pps, hd, n_heads, page, past_len, lam_init):
    page_refs = rest[:2 * pps]
    o_ref = rest[2 * pps]
    qs, kf_s, vf_s, m_s, l_s, a_s = rest[2 * pps + 1:]
    p = pl.program_id(1)
    rows = q_ref.shape[0]
    prow = page * n_heads

    @pl.when(p == 0)
    def _():
        q = (q_ref[...] * (hd ** -0.5 * LOG2E)).astype(BF16)
        zero = jnp.zeros((rows, hd), BF16)
        qs[0:rows, :] = jnp.concatenate([q[:, :hd], zero], axis=1)
        qs[rows:, :] = jnp.concatenate([zero, q[:, hd:]], axis=1)
        m_s[...] = jnp.full(m_s.shape, NEG_BIG, F32)
        l_s[...] = jnp.zeros(l_s.shape, F32)
        a_s[...] = jnp.zeros(a_s.shape, F32)

    row = lax.broadcasted_iota(jnp.int32, (2 * rows, 1), 0) % rows
    t_row = row // n_heads
    h_row = row % n_heads
    slope = jnp.concatenate([slope_ref[...], slope_ref[...]], axis=0)

    def attend(kf, vf, first_pos, causal):
        cols = kf.shape[0]
        col = lax.broadcasted_iota(jnp.int32, (1, cols), 1)
        tok = col // n_heads
        valid = h_row == col % n_heads
        if causal:
            valid = jnp.logical_and(valid, tok <= t_row)
        bias = slope * (first_pos - past_len + tok).astype(F32)
        sc = lax.dot_general(qs[...], kf, NT_DIMS, preferred_element_type=F32) + bias
        _softmax_update(jnp.where(valid, sc, NEG_BIG), vf, m_s, l_s, a_s)

    for i in range(pps):
        kf_s[i * prow:(i + 1) * prow, :] = page_refs[i][...].reshape(prow, 2 * hd).astype(BF16)
        vf_s[i * prow:(i + 1) * prow, :] = page_refs[pps + i][...].reshape(prow, 2 * hd).astype(BF16)
    attend(kf_s[...], vf_s[...], p * pps * page, False)

    @pl.when(p == pl.num_programs(1) - 1)
    def _():
        attend(kn_ref[...].astype(BF16), vn_ref[...].astype(BF16), past_len, True)
        top, bot = pl.ds(0, rows), pl.ds(rows, rows)
        _diff_finalize(lam_ref, g_ref, o_ref, l_s.at[top], a_s.at[top], l_s.at[bot], a_s.at[bot], lam_init)


def _attn_sample(q, k_new, v_new, cache_k, cache_v, layer, page_table, lam_vecs, subln, slopes,
                 n_heads, hd, lam_init):
    nb, rows, hw = q.shape
    page = cache_k.shape[2]
    n_pages = page_table.shape[1]
    pps = _pick_tile(n_pages, (8, 4, 2, 1))
    kern = functools.partial(_attn_sample_kernel, pps=pps, hd=hd, n_heads=n_heads, page=page,
                             past_len=n_pages * page, lam_init=lam_init)
    slope_rows = jnp.tile(slopes, rows // n_heads).reshape(rows, 1)

    def page_spec(i):
        return pl.BlockSpec((None, None, page, n_heads, hw),
                            lambda b, p, pt: (layer, pt[b * n_pages + p * pps + i], 0, 0, 0))

    row_spec = pl.BlockSpec((None, rows, hw), lambda b, p, pt: (b, 0, 0))
    grid_spec = pltpu.PrefetchScalarGridSpec(
        num_scalar_prefetch=1,
        grid=(nb, n_pages // pps),
        in_specs=[row_spec, row_spec, row_spec,
                  pl.BlockSpec((rows, 1), lambda b, p, pt: (0, 0)),
                  pl.BlockSpec((4, hd), lambda b, p, pt: (0, 0)),
                  pl.BlockSpec((1, hw), lambda b, p, pt: (0, 0))]
                 + [page_spec(i) for i in range(pps)] * 2,
        out_specs=row_spec,
        scratch_shapes=[pltpu.VMEM((2 * rows, hw), BF16),
                        pltpu.VMEM((pps * page * n_heads, hw), BF16),
                        pltpu.VMEM((pps * page * n_heads, hw), BF16),
                        pltpu.VMEM((2 * rows, 1), F32), pltpu.VMEM((2 * rows, 1), F32),
                        pltpu.VMEM((2 * rows, hw), F32)])
    return pl.pallas_call(
        kern, grid_spec=grid_spec,
        out_shape=jax.ShapeDtypeStruct((nb, rows, hw), BF16),
        compiler_params=_cparams(2),
        name="attn_sample",
    )(page_table.reshape(-1), q, k_new, v_new, slope_rows, lam_vecs, subln.reshape(1, hw),
      *([cache_k] * pps), *([cache_v] * pps))


def _s5_discretize(lam_re, lam_im, log_dt, b_re, b_im):
    dt = jnp.exp(log_dt)[:, None]
    mag = jnp.exp(lam_re * dt)
    a_re = mag * jnp.cos(lam_im * dt)
    a_im = mag * jnp.sin(lam_im * dt)
    den = lam_re * lam_re + lam_im * lam_im
    nr = a_re - 1.0
    coef_re = (nr * lam_re + a_im * lam_im) / den
    coef_im = (a_im * lam_re - nr * lam_im) / den
    bb_re = coef_re[..., None] * b_re - coef_im[..., None] * b_im
    bb_im = coef_re[..., None] * b_im + coef_im[..., None] * b_re
    return a_re, a_im, bb_re, bb_im


def _s5_weights(lam_re, lam_im, log_dt, b_re, b_im, c_re, c_im):
    g, p = lam_re.shape
    c = b_re.shape[2]
    gpt = LANES // c
    nt = g // gpt
    a_re, a_im, bb_re, bb_im = _s5_discretize(lam_re, lam_im, log_dt, b_re, b_im)
    eye = jnp.eye(gpt, dtype=F32)

    def in_proj(bb):
        t = bb.reshape(nt, gpt, p, c)
        return jnp.einsum('jgpc,gh->jgchp', t, eye).reshape(nt, gpt * c, gpt * p)

    def out_proj(cm):
        t = cm.reshape(nt, gpt, c, p)
        return jnp.einsum('jgcp,gh->jgphc', t, eye).reshape(nt, gpt * p, gpt * c)

    wb = jnp.concatenate([in_proj(bb_re), in_proj(bb_im)], axis=2).astype(BF16)
    cc = jnp.concatenate([out_proj(c_re), -out_proj(c_im)], axis=1).astype(BF16)
    return a_re.reshape(-1), a_im.reshape(-1), wb, cc


def _s5_prompt_kernel(u_ref, wb_ref, cc_ref, are_ref, aim_ref, d_ref, y_ref, sre_ref, sim_ref,
                      bre, bim, pre, pim, cre, cim, *, tt, seg, last_row, spt):
    i = pl.program_id(0)
    n_slabs = bre.shape[0]
    n_tiles = u_ref.shape[1] // LANES

    @pl.when(i == 0)
    def _():
        cre[...] = jnp.zeros(cre.shape, F32)
        cim[...] = jnp.zeros(cim.shape, F32)
        a_r = are_ref[...]
        a_i = aim_ref[...]
        p_r, p_i = a_r, a_i
        for k in range(seg):
            pre[:, k:k + 1, :] = p_r
            pim[:, k:k + 1, :] = p_i
            p_r, p_i = p_r * a_r - p_i * a_i, p_r * a_i + p_i * a_r

    u = u_ref[...]
    for j in range(n_tiles):
        r = jnp.dot(u[:, j * LANES:(j + 1) * LANES].astype(BF16), wb_ref[j], preferred_element_type=F32)
        for q in range(spt):
            bre[j * spt + q] = r[:, q * LANES:(q + 1) * LANES]
            bim[j * spt + q] = r[:, (spt + q) * LANES:(spt + q + 1) * LANES]

    def slab_body(s, carry):
        a_r = jnp.broadcast_to(are_ref[s], (SUBLANES, LANES))
        a_i = jnp.broadcast_to(aim_ref[s], (SUBLANES, LANES))
        x_r = jnp.zeros((SUBLANES, LANES), F32)
        x_i = jnp.zeros((SUBLANES, LANES), F32)
        local = []
        for k in range(seg):
            b_r = bre[s, pl.ds(k, SUBLANES, stride=seg), :]
            b_i = bim[s, pl.ds(k, SUBLANES, stride=seg), :]
            x_r, x_i = a_r * x_r - a_i * x_i + b_r, a_r * x_i + a_i * x_r + b_i
            local.append((x_r, x_i))
        as_r = pre[s, seg - 1:seg, :]
        as_i = pim[s, seg - 1:seg, :]
        s_r = cre[s]
        s_i = cim[s]
        st_r, st_i = [], []
        for g in range(SUBLANES):
            st_r.append(s_r)
            st_i.append(s_i)
            e_r = x_r[g:g + 1]
            e_i = x_i[g:g + 1]
            s_r, s_i = as_r * s_r - as_i * s_i + e_r, as_r * s_i + as_i * s_r + e_i
        cre[s] = s_r
        cim[s] = s_i
        st_r = jnp.concatenate(st_r, axis=0)
        st_i = jnp.concatenate(st_i, axis=0)
        for k in range(seg):
            p_r = jnp.broadcast_to(pre[s, k:k + 1, :], (SUBLANES, LANES))
            p_i = jnp.broadcast_to(pim[s, k:k + 1, :], (SUBLANES, LANES))
            l_r, l_i = local[k]
            bre[s, pl.ds(k, SUBLANES, stride=seg), :] = l_r + p_r * st_r - p_i * st_i
            bim[s, pl.ds(k, SUBLANES, stride=seg), :] = l_i + p_r * st_i + p_i * st_r
        return carry

    lax.fori_loop(0, n_slabs, slab_body, 0)

    for j in range(n_tiles):
        xcat = jnp.concatenate([bre[j * spt + q] for q in range(spt)]
                               + [bim[j * spt + q] for q in range(spt)], axis=1).astype(BF16)
        sl = slice(j * LANES, (j + 1) * LANES)
        y = jnp.dot(xcat, cc_ref[j], preferred_element_type=F32) + d_ref[:, sl] * u[:, sl]
        y_ref[:, sl] = _gelu(y).astype(y_ref.dtype)

    @pl.when(i == pl.num_programs(0) - 1)
    def _():
        sre_ref[...] = bre[:, last_row:last_row + 1, :]
        sim_ref[...] = bim[:, last_row:last_row + 1, :]


def _s5_prompt(u, a_re, a_im, wb, cc, d, n_valid, tt):
    n, dm = u.shape
    nt, _, two_w = wb.shape
    spt = two_w // 2 // LANES
    n_slabs = nt * spt
    seg = tt // SUBLANES
    nblk = n // tt
    assert (n_valid - 1) // tt == nblk - 1
    kern = functools.partial(_s5_prompt_kernel, tt=tt, seg=seg, last_row=(n_valid - 1) % tt, spt=spt)
    slab = jax.ShapeDtypeStruct((n_slabs, 1, LANES), F32)
    const3 = lambda i: (0, 0, 0)
    y, sre, sim = pl.pallas_call(
        kern, grid=(nblk,),
        in_specs=[pl.BlockSpec((tt, dm), lambda i: (i, 0)),
                  pl.BlockSpec(wb.shape, const3),
                  pl.BlockSpec(cc.shape, const3),
                  pl.BlockSpec((n_slabs, 1, LANES), const3),
                  pl.BlockSpec((n_slabs, 1, LANES), const3),
                  pl.BlockSpec((1, dm), lambda i: (0, 0))],
        out_specs=[pl.BlockSpec((tt, dm), lambda i: (i, 0)),
                   pl.BlockSpec((n_slabs, 1, LANES), const3),
                   pl.BlockSpec((n_slabs, 1, LANES), const3)],
        out_shape=[jax.ShapeDtypeStruct((n, dm), BF16), slab, slab],
        scratch_shapes=[pltpu.VMEM((n_slabs, tt, LANES), F32), pltpu.VMEM((n_slabs, tt, LANES), F32),
                        pltpu.VMEM((n_slabs, seg, LANES), F32), pltpu.VMEM((n_slabs, seg, LANES), F32),
                        pltpu.VMEM((n_slabs, 1, LANES), F32), pltpu.VMEM((n_slabs, 1, LANES), F32)],
        compiler_params=_cparams(1),
        name="s5_prompt",
    )(u, wb, cc, a_re.reshape(n_slabs, 1, LANES), a_im.reshape(n_slabs, 1, LANES), d.reshape(1, dm))
    return y, sre.reshape(-1), sim.reshape(-1)


def _s5_sample_kernel(u_ref, wb_ref, cc_ref, are_ref, aim_ref, d_ref, s0r_ref, s0i_ref,
                      y_ref, sre_ref, sim_ref, bre, bim):
    n_steps = u_ref.shape[0]
    n_tiles = u_ref.shape[2] // LANES
    half = wb_ref.shape[2] // 2
    a_r = are_ref[...]
    a_i = aim_ref[...]
    x_r = s0r_ref[...]
    x_i = s0i_ref[...]
    for t in range(n_steps):
        u = u_ref[t]
        for j in range(n_tiles):
            r = jnp.dot(u[:, j * LANES:(j + 1) * LANES].astype(BF16), wb_ref[j], preferred_element_type=F32)
            bre[:, j * half:(j + 1) * half] = r[:, :half]
            bim[:, j * half:(j + 1) * half] = r[:, half:]
        x_r, x_i = a_r * x_r - a_i * x_i + bre[...], a_r * x_i + a_i * x_r + bim[...]
        for j in range(n_tiles):
            xcat = jnp.concatenate([x_r[:, j * half:(j + 1) * half], x_i[:, j * half:(j + 1) * half]],
                                   axis=1).astype(BF16)
            sl = slice(j * LANES, (j + 1) * LANES)
            y = jnp.dot(xcat, cc_ref[j], preferred_element_type=F32) + d_ref[:, sl] * u[:, sl]
            y_ref[t, :, sl] = _gelu(y).astype(y_ref.dtype)
    sre_ref[...] = x_r
    sim_ref[...] = x_i


def _s5_sample(u, a_re, a_im, wb, cc, d, s0_re, s0_im):
    n_steps, nb, dm = u.shape
    width = s0_re.shape[1]
    tb = _pick_tile(nb, (32, 16, 8))
    const3 = lambda i: (0, 0, 0)
    st = jax.ShapeDtypeStruct((nb, width), F32)
    return pl.pallas_call(
        _s5_sample_kernel, grid=(nb // tb,),
        in_specs=[pl.BlockSpec((n_steps, tb, dm), lambda i: (0, i, 0)),
                  pl.BlockSpec(wb.shape, const3),
                  pl.BlockSpec(cc.shape, const3),
                  pl.BlockSpec((1, width), lambda i: (0, 0)),
                  pl.BlockSpec((1, width), lambda i: (0, 0)),
                  pl.BlockSpec((1, dm), lambda i: (0, 0)),
                  pl.BlockSpec((tb, width), lambda i: (i, 0)),
                  pl.BlockSpec((tb, width), lambda i: (i, 0))],
        out_specs=[pl.BlockSpec((n_steps, tb, dm), lambda i: (0, i, 0)),
                   pl.BlockSpec((tb, width), lambda i: (i, 0)),
                   pl.BlockSpec((tb, width), lambda i: (i, 0))],
        out_shape=[jax.ShapeDtypeStruct((n_steps, nb, dm), BF16), st, st],
        scratch_shapes=[pltpu.VMEM((tb, width), F32), pltpu.VMEM((tb, width), F32)],
        compiler_params=_cparams(1),
        name="s5_sample",
    )(u, wb, cc, a_re.reshape(1, width), a_im.reshape(1, width), d.reshape(1, dm), s0_re, s0_im)


def _topk_rows(s, k):
    n = s.shape[0]
    row = lax.broadcasted_iota(jnp.int32, s.shape, 0)
    vals, idxs = [], []
    for _ in range(k):
        m = jnp.max(s, axis=0, keepdims=True)
        idx = jnp.min(jnp.where(s == m, row, n), axis=0, keepdims=True)
        vals.append(m)
        idxs.append(idx)
        s = jnp.where(row == idx, -jnp.inf, s)
    return jnp.concatenate(vals, axis=0), jnp.concatenate(idxs, axis=0)


def _candidate_pairs(k):
    return [(i, j) for i in range(k) for j in range(k) if (i + 1) * (j + 1) <= k]


def _peer_router_kernel(x_ref, g_ref, wq_ref, keys_ref, h_ref, gt_ref, i1_ref, i2_ref, q_s,
                        *, n_heads, topk):
    x = x_ref[...]
    hb = (x * _rms_scale(x) * g_ref[...]).astype(BF16)
    h_ref[...] = hb
    q_s[...] = jnp.dot(hb, wq_ref[...], preferred_element_type=F32).astype(BF16)
    tm = x.shape[0]
    half = keys_ref.shape[2]
    n_chunks = tm // LANES
    pairs = _candidate_pairs(topk)
    counts = [sum(1 for p in pairs if p[0] == i) for i in range(topk)]
    n_cand = _round_up(len(pairs), SUBLANES)

    def body(it, carry):
        hd = it % n_heads
        r0 = pl.multiple_of((it // n_heads) * LANES, LANES)
        q = q_s[pl.ds(r0, LANES), pl.ds(pl.multiple_of(hd * 2 * half, 2 * half), 2 * half)]
        s0 = lax.dot_general(keys_ref[0], q[:, :half], NT_DIMS, preferred_element_type=F32)
        s1 = lax.dot_general(keys_ref[1], q[:, half:], NT_DIMS, preferred_element_type=F32)
        sv0, si0 = _topk_rows(s0, topk)
        sv1, si1 = _topk_rows(s1, topk)
        si0 = si0.astype(F32)
        si1 = si1.astype(F32)
        cs, c1, c2 = [], [], []
        for i in range(topk):
            cnt = counts[i]
            cs.append(sv0[i:i + 1] + sv1[0:cnt])
            c1.append(jnp.broadcast_to(si0[i:i + 1], (cnt, LANES)))
            c2.append(si1[0:cnt])
        pad = n_cand - len(pairs)
        if pad:
            cs.append(jnp.full((pad, LANES), -jnp.inf, F32))
            c1.append(jnp.zeros((pad, LANES), F32))
            c2.append(jnp.zeros((pad, LANES), F32))
        cand = jnp.concatenate(cs, axis=0)
        cand1 = jnp.concatenate(c1, axis=0)
        cand2 = jnp.concatenate(c2, axis=0)
        row = lax.broadcasted_iota(jnp.int32, cand.shape, 0)
        ts, t1, t2 = [], [], []
        for _ in range(topk):
            m = jnp.max(cand, axis=0, keepdims=True)
            idx = jnp.min(jnp.where(cand == m, row, n_cand), axis=0, keepdims=True)
            sel = row == idx
            ts.append(m)
            t1.append(jnp.sum(jnp.where(sel, cand1, 0.0), axis=0, keepdims=True))
            t2.append(jnp.sum(jnp.where(sel, cand2, 0.0), axis=0, keepdims=True))
            cand = jnp.where(sel, -jnp.inf, cand)
        ts = jnp.concatenate(ts, axis=0)
        e = jnp.exp(ts - ts[0:1])
        gate = e / jnp.sum(e, axis=0, keepdims=True)
        rows = pl.ds(pl.multiple_of(hd * topk, topk), topk)
        cols = pl.ds(r0, LANES)
        gt_ref[rows, cols] = gate
        i1_ref[rows, cols] = jnp.concatenate(t1, axis=0)
        i2_ref[rows, cols] = jnp.concatenate(t2, axis=0)
        return carry

    lax.fori_loop(0, n_heads * n_chunks, body, 0, unroll=2)


def _peer_router(x, g, w_q, sub_keys):
    n, d = x.shape
    n_keys, half = sub_keys.shape[1], sub_keys.shape[2]
    assert n_keys == LANES and half == LANES
    n_heads = w_q.shape[1] // (2 * half)
    hk = n_heads * PEER_TOPK
    tm = _pick_tile(n, (640, 512, 256, 128))
    kern = functools.partial(_peer_router_kernel, n_heads=n_heads, topk=PEER_TOPK)
    tr = jax.ShapeDtypeStruct((hk, n), F32)
    return pl.pallas_call(
        kern, grid=(n // tm,),
        in_specs=[pl.BlockSpec((tm, d), lambda i: (i, 0)),
                  pl.BlockSpec((1, d), lambda i: (0, 0)),
                  pl.BlockSpec(w_q.shape, lambda i: (0, 0)),
                  pl.BlockSpec(sub_keys.shape, lambda i: (0, 0, 0))],
        out_specs=[pl.BlockSpec((tm, d), lambda i: (i, 0)),
                   pl.BlockSpec((hk, tm), lambda i: (0, i)),
                   pl.BlockSpec((hk, tm), lambda i: (0, i)),
                   pl.BlockSpec((hk, tm), lambda i: (0, i))],
        out_shape=[jax.ShapeDtypeStruct((n, d), BF16), tr, tr, tr],
        scratch_shapes=[pltpu.VMEM((tm, w_q.shape[1]), BF16)],
        compiler_params=_cparams(1),
        name="peer_router",
    )(x, g.reshape(1, d), w_q, sub_keys)


def _peer_expand_kernel(gt_ref, i1_ref, i2_ref, w_ref, g_s, a_s, b_s, scr, *, pitch):
    g_s[...] = gt_ref[...].T
    a_s[...] = i1_ref[...].T
    b_s[...] = i2_ref[...].T
    tn, hk = g_s.shape
    n_keys = w_ref.shape[1] // LANES
    sub = lax.broadcasted_iota(jnp.int32, (n_keys, hk), 0).astype(F32)

    def body(n, carry):
        at = jnp.where(a_s[pl.ds(n, 1), :] == sub, g_s[pl.ds(n, 1), :], 0.0).astype(BF16)
        bt = jnp.where(b_s[pl.ds(n, 1), :] == sub, 1.0, 0.0).astype(BF16)
        scr[pl.ds(pl.multiple_of(n * pitch, SUBLANES), n_keys), :] = lax.dot_general(
            at, bt, NT_DIMS, preferred_element_type=F32)
        return carry

    lax.fori_loop(0, tn, body, 0, unroll=8)
    for a in range(n_keys):
        w_ref[:, a * LANES:(a + 1) * LANES] = scr[pl.ds(a, tn, stride=pitch), :].astype(w_ref.dtype)


def _peer_expand(gt, i1, i2, n_keys):
    hk, n = gt.shape
    assert n_keys == LANES and hk == LANES
    tn = LANES
    pitch = n_keys + SUBLANES
    kern = functools.partial(_peer_expand_kernel, pitch=pitch)
    spec = pl.BlockSpec((hk, tn), lambda i: (0, i))
    return pl.pallas_call(
        kern, grid=(n // tn,),
        in_specs=[spec, spec, spec],
        out_specs=pl.BlockSpec((tn, n_keys * n_keys), lambda i: (i, 0)),
        out_shape=jax.ShapeDtypeStruct((n, n_keys * n_keys), BF16),
        scratch_shapes=[pltpu.VMEM((tn, hk), F32), pltpu.VMEM((tn, hk), F32), pltpu.VMEM((tn, hk), F32),
                        pltpu.VMEM((tn * pitch, LANES), F32)],
        compiler_params=_cparams(1),
        name="peer_expand",
    )(gt, i1, i2)


def _peer_dense_kernel(h_ref, u_ref, v_ref, w_ref, r_ref, o_ref):
    @pl.when(pl.program_id(1) == 0)
    def _():
        o_ref[...] = r_ref[...]

    act = lax.dot_general(h_ref[...], u_ref[...], NT_DIMS, preferred_element_type=F32)
    p = (w_ref[...].astype(F32) * _gelu(act)).astype(BF16)
    o_ref[...] += jnp.dot(p, v_ref[...], preferred_element_type=F32)


def _peer_dense(h, u_tab, v_tab, w, res):
    n, d = h.shape
    n_exp = u_tab.shape[0]
    tm = _pick_tile(n, (640, 512, 256, 128))
    te = 1024
    once = pl.Buffered(1)
    return pl.pallas_call(
        _peer_dense_kernel, grid=(n // tm, n_exp // te),
        in_specs=[pl.BlockSpec((tm, d), lambda i, e: (i, 0), pipeline_mode=once),
                  pl.BlockSpec((te, d), lambda i, e: (e, 0)),
                  pl.BlockSpec((te, d), lambda i, e: (e, 0)),
                  pl.BlockSpec((tm, te), lambda i, e: (i, e)),
                  pl.BlockSpec((tm, d), lambda i, e: (i, 0), pipeline_mode=once)],
        out_specs=pl.BlockSpec((tm, d), lambda i, e: (i, 0)),
        out_shape=jax.ShapeDtypeStruct((n, d), F32),
        compiler_params=_cparams(2),
        name="peer_dense",
    )(h, u_tab, v_tab, w, res)


def _peer_ffn(x, g, w_q, sub_keys, u_tab, v_tab):
    h, gt, i1, i2 = _peer_router(x, g, w_q, sub_keys)
    w = _peer_expand(gt, i1, i2, sub_keys.shape[1])
    return _peer_dense(h, u_tab, v_tab, w, x)


def kernel(x_prompt, x_sample, cache_k, cache_v, state_ssm_re, state_ssm_im, page_table, meta_tokens, norm_mix, norm_ffn, norm_final, attn_w_qkv, attn_lambda, attn_subln, attn_w_o, ssm_w_in, ssm_lambda_re, ssm_lambda_im, ssm_log_dt, ssm_b_re, ssm_b_im, ssm_c_re, ssm_c_im, ssm_d, ssm_w_glu, peer_w_q, peer_sub_keys, peer_u, peer_v):
    batch, seq, d_model = x_prompt.shape
    assert batch == 1
    dec_batch, dec_seq, _ = x_sample.shape
    n_meta = meta_tokens.shape[0]
    depth = norm_mix.shape[0]
    n_heads, hw = cache_k.shape[3], cache_k.shape[4]
    hd = hw // 2
    n_groups, state_dim = ssm_lambda_re.shape[1], ssm_lambda_re.shape[2]
    t_tot = n_meta + seq
    tq = 640
    tt = 160
    n_p = _round_up(t_tot, math.lcm(tq, tt))
    n_s = dec_batch * dec_seq
    slopes = jnp.asarray(2.0 ** (-8.0 * np.arange(1, n_heads + 1) / n_heads) * LOG2E, dtype=F32)

    xp = jnp.concatenate([meta_tokens.astype(F32), x_prompt[0],
                          jnp.zeros((n_p - t_tot, d_model), F32)], axis=0)
    xs = x_sample.reshape(n_s, d_model)

    outs = {k: [] for k in ("kp", "vp", "ks", "vs", "srp", "sip", "srs", "sis")}
    for i in range(depth):
        j = i // 2
        if i % 2 == 0:
            lam_init = _lambda_init(i)
            w_qkv = attn_w_qkv[j].astype(BF16)
            w_o = attn_w_o[j].astype(BF16)
            qk_w = n_heads * hw
            qkv_p = _norm_matmul(xp, norm_mix[i], w_qkv)
            qkv_s = _norm_matmul(xs, norm_mix[i], w_qkv)
            o_p = _attn_prompt(qkv_p, attn_lambda[j], attn_subln[j], slopes, n_heads, hd, lam_init, tq)

            def rows(a):
                return a.reshape(dec_batch, dec_seq * n_heads, hw)

            o_s = _attn_sample(rows(qkv_s[:, :qk_w]), rows(qkv_s[:, qk_w:2 * qk_w]), rows(qkv_s[:, 2 * qk_w:]),
                               cache_k, cache_v, j, page_table, attn_lambda[j], attn_subln[j], slopes,
                               n_heads, hd, lam_init)
            xp = _matmul_residual(o_p, w_o, xp)
            xs = _matmul_residual(o_s.reshape(n_s, qk_w), w_o, xs)
            outs["kp"].append(qkv_p[:t_tot, qk_w:2 * qk_w].reshape(batch, t_tot, n_heads, hw))
            outs["vp"].append(qkv_p[:t_tot, 2 * qk_w:].reshape(batch, t_tot, n_heads, hw))
            outs["ks"].append(qkv_s[:, qk_w:2 * qk_w].reshape(dec_batch, dec_seq, n_heads, hw))
            outs["vs"].append(qkv_s[:, 2 * qk_w:].reshape(dec_batch, dec_seq, n_heads, hw))
        else:
            w_in = ssm_w_in[j].astype(BF16)
            w_glu = ssm_w_glu[j].astype(BF16)
            a_re, a_im, wb, cc = _s5_weights(ssm_lambda_re[j], ssm_lambda_im[j], ssm_log_dt[j],
                                             ssm_b_re[j], ssm_b_im[j], ssm_c_re[j], ssm_c_im[j])
            d_vec = ssm_d[j].reshape(-1)
            u_p = _norm_matmul(xp, norm_mix[i], w_in)
            y_p, sr_p, si_p = _s5_prompt(u_p, a_re, a_im, wb, cc, d_vec, t_tot, tt)
            xp = _glu_residual(y_p, w_glu, xp)
            u_s = _norm_matmul(xs, norm_mix[i], w_in)
            u_s = u_s.reshape(dec_batch, dec_seq, d_model).transpose(1, 0, 2)
            y_s, sr_s, si_s = _s5_sample(u_s, a_re, a_im, wb, cc, d_vec,
                                         state_ssm_re[j].reshape(dec_batch, -1),
                                         state_ssm_im[j].reshape(dec_batch, -1))
            xs = _glu_residual(y_s.transpose(1, 0, 2).reshape(n_s, d_model), w_glu, xs)
            outs["srp"].append(sr_p.reshape(batch, n_groups, state_dim))
            outs["sip"].append(si_p.reshape(batch, n_groups, state_dim))
            outs["srs"].append(sr_s.reshape(dec_batch, n_groups, state_dim))
            outs["sis"].append(si_s.reshape(dec_batch, n_groups, state_dim))
        w_q = peer_w_q[i].astype(BF16)
        keys = peer_sub_keys[i].astype(BF16)
        u_tab = peer_u[i].astype(BF16)
        v_tab = peer_v[i].astype(BF16)
        xp = _peer_ffn(xp, norm_ffn[i], w_q, keys, u_tab, v_tab)
        xs = _peer_ffn(xs, norm_ffn[i], w_q, keys, u_tab, v_tab)

    y_prompt = _final_norm(xp, norm_final)[n_meta:t_tot].reshape(batch, seq, d_model)
    y_sample = _final_norm(xs, norm_final).reshape(dec_batch, dec_seq, d_model)
    return (y_prompt, y_sample,
            jnp.stack(outs["kp"]), jnp.stack(outs["vp"]), jnp.stack(outs["srp"]), jnp.stack(outs["sip"]),
            jnp.stack(outs["ks"]), jnp.stack(outs["vs"]), jnp.stack(outs["srs"]), jnp.stack(outs["sis"]))
```

```python
import functools
import math

import numpy as np
import jax
import jax.numpy as jnp
from jax import lax
from jax.experimental import pallas as pl
from jax.experimental.pallas import tpu as pltpu

F32 = jnp.float32
BF16 = jnp.bfloat16
EPS = 1e-6
PEER_TOPK = 16
LANES = 128
SUBLANES = 8
NEG_BIG = -1e30
LOG2E = math.log2(math.e)
VMEM_LIMIT = 56 * 1024 * 1024
NT_DIMS = (((1,), (1,)), ((), ()))


def _cparams(n_axes):
    return pltpu.CompilerParams(dimension_semantics=("arbitrary",) * n_axes,
                                vmem_limit_bytes=VMEM_LIMIT)


def _round_up(x, m):
    return (x + m - 1) // m * m


def _pick_tile(n, candidates):
    for c in candidates:
        if n % c == 0:
            return c
    return n


def _gelu(x):
    return 0.5 * x * (1.0 + lax.erf(x * (1.0 / math.sqrt(2.0))))


def _rms_scale(x):
    return lax.rsqrt(jnp.mean(x * x, axis=-1, keepdims=True) + EPS)


def _lambda_init(layer):
    return 0.8 - 0.6 * math.exp(-0.3 * layer)


def _norm_mm_kernel(x_ref, g_ref, w_ref, o_ref, h_ref):
    @pl.when(pl.program_id(1) == 0)
    def _():
        x = x_ref[...]
        h_ref[...] = (x * _rms_scale(x) * g_ref[...]).astype(BF16)

    o_ref[...] = jnp.dot(h_ref[...], w_ref[...], preferred_element_type=F32)


def _norm_matmul(x, g, w):
    n, d = x.shape
    m = w.shape[1]
    tm = _pick_tile(n, (640, 512, 256, 128))
    tn = _pick_tile(m, (1024, 512, 256, 128))
    return pl.pallas_call(
        _norm_mm_kernel,
        grid=(n // tm, m // tn),
        in_specs=[pl.BlockSpec((tm, d), lambda i, j: (i, 0)),
                  pl.BlockSpec((1, d), lambda i, j: (0, 0)),
                  pl.BlockSpec((d, tn), lambda i, j: (0, j))],
        out_specs=pl.BlockSpec((tm, tn), lambda i, j: (i, j)),
        out_shape=jax.ShapeDtypeStruct((n, m), F32),
        scratch_shapes=[pltpu.VMEM((tm, d), BF16)],
        compiler_params=_cparams(2),
        name="norm_matmul",
    )(x, g.reshape(1, d), w)


def _mm_res_kernel(a_ref, w_ref, r_ref, o_ref):
    o_ref[...] = r_ref[...] + jnp.dot(a_ref[...], w_ref[...], preferred_element_type=F32)


def _matmul_residual(a, w, res):
    n, k = a.shape
    m = w.shape[1]
    tm = _pick_tile(n, (640, 512, 256, 128))
    tn = _pick_tile(m, (1024, 512, 256, 128))
    return pl.pallas_call(
        _mm_res_kernel,
        grid=(n // tm, m // tn),
        in_specs=[pl.BlockSpec((tm, k), lambda i, j: (i, 0)),
                  pl.BlockSpec((k, tn), lambda i, j: (0, j)),
                  pl.BlockSpec((tm, tn), lambda i, j: (i, j))],
        out_specs=pl.BlockSpec((tm, tn), lambda i, j: (i, j)),
        out_shape=jax.ShapeDtypeStruct((n, m), F32),
        compiler_params=_cparams(2),
        name="matmul_residual",
    )(a, w, res)


def _glu_res_kernel(a_ref, wv_ref, wg_ref, r_ref, o_ref):
    a = a_ref[...]
    zv = jnp.dot(a, wv_ref[...], preferred_element_type=F32)
    zg = jnp.dot(a, wg_ref[...], preferred_element_type=F32)
    o_ref[...] = r_ref[...] + zv * jax.nn.sigmoid(zg)


def _glu_residual(a, w, res):
    n, k = a.shape
    m = w.shape[1] // 2
    tm = _pick_tile(n, (640, 512, 256, 128))
    tn = _pick_tile(m, (512, 256, 128))
    nj = m // tn
    return pl.pallas_call(
        _glu_res_kernel,
        grid=(n // tm, nj),
        in_specs=[pl.BlockSpec((tm, k), lambda i, j: (i, 0)),
                  pl.BlockSpec((k, tn), lambda i, j: (0, j)),
                  pl.BlockSpec((k, tn), lambda i, j: (0, j + nj)),
                  pl.BlockSpec((tm, tn), lambda i, j: (i, j))],
        out_specs=pl.BlockSpec((tm, tn), lambda i, j: (i, j)),
        out_shape=jax.ShapeDtypeStruct((n, m), F32),
        compiler_params=_cparams(2),
        name="glu_residual",
    )(a, w, w, res)


def _final_norm_kernel(x_ref, g_ref, o_ref):
    x = x_ref[...]
    o_ref[...] = x * _rms_scale(x) * g_ref[...]


def _final_norm(x, g):
    n, d = x.shape
    tm = _pick_tile(n, (640, 512, 256, 128))
    return pl.pallas_call(
        _final_norm_kernel,
        grid=(n // tm,),
        in_specs=[pl.BlockSpec((tm, d), lambda i: (i, 0)),
                  pl.BlockSpec((1, d), lambda i: (0, 0))],
        out_specs=pl.BlockSpec((tm, d), lambda i: (i, 0)),
        out_shape=jax.ShapeDtypeStruct((n, d), F32),
        compiler_params=_cparams(1),
        name="final_norm",
    )(x, g.reshape(1, d))


def _diff_lambda(lam_ref, lam_init):
    lv = lam_ref[...]
    s1 = jnp.sum(lv[0:1] * lv[1:2], axis=1, keepdims=True)
    s2 = jnp.sum(lv[2:3] * lv[3:4], axis=1, keepdims=True)
    return jnp.exp(s1) - jnp.exp(s2) + lam_init


def _softmax_update(sc, v, m_ref, l_ref, a_ref):
    m_old = m_ref[...]
    m_new = jnp.maximum(m_old, jnp.max(sc, axis=1, keepdims=True))
    alpha = jnp.exp2(m_old - m_new)
    p = jnp.exp2(sc - m_new)
    l_ref[...] = alpha * l_ref[...] + jnp.sum(p, axis=1, keepdims=True)
    a_ref[...] = alpha * a_ref[...] + jnp.dot(p.astype(BF16), v, preferred_element_type=F32)
    m_ref[...] = m_new


def _diff_finalize(lam_ref, g_ref, o_ref, l1, a1, l2, a2, lam_init):
    lam = _diff_lambda(lam_ref, lam_init)
    o = a1[...] / l1[...] - lam * (a2[...] / l2[...])
    o_ref[...] = ((o * _rms_scale(o) * g_ref[...]) * (1.0 - lam_init)).astype(o_ref.dtype)


def _init_softmax_state(m1, l1, a1, m2, l2, a2):
    for m, l, a in ((m1, l1, a1), (m2, l2, a2)):
        m[...] = jnp.full(m.shape, NEG_BIG, F32)
        l[...] = jnp.zeros(l.shape, F32)
        a[...] = jnp.zeros(a.shape, F32)


def _flash_prompt_kernel(qi_ref, kj_ref, slope_ref, q_ref, k_ref, v_ref, lam_ref, g_ref, o_ref,
                         qs, m1, l1, a1, m2, l2, a2, *, tq, hd, lam_init):
    h = pl.program_id(0)
    s = pl.program_id(1)
    qi = qi_ref[s]
    kj = kj_ref[s]

    @pl.when(kj == 0)
    def _():
        qs[...] = (q_ref[...] * (hd ** -0.5 * LOG2E)).astype(BF16)
        _init_softmax_state(m1, l1, a1, m2, l2, a2)

    def update(masked):
        k = k_ref[...].astype(BF16)
        v = v_ref[...].astype(BF16)
        col = lax.broadcasted_iota(jnp.int32, (1, tq), 1)
        bias = slope_ref[h] * ((kj - qi) * tq + col).astype(F32)
        if masked:
            keep = (lax.broadcasted_iota(jnp.int32, (tq, tq), 1)
                    <= lax.broadcasted_iota(jnp.int32, (tq, tq), 0))
        for lo, m_ref, l_ref, a_ref in ((0, m1, l1, a1), (hd, m2, l2, a2)):
            sc = lax.dot_general(qs[:, lo:lo + hd], k[:, lo:lo + hd], NT_DIMS,
                                 preferred_element_type=F32) + bias
            if masked:
                sc = jnp.where(keep, sc, NEG_BIG)
            _softmax_update(sc, v, m_ref, l_ref, a_ref)

    @pl.when(kj < qi)
    def _():
        update(False)

    @pl.when(kj == qi)
    def _():
        update(True)
        _diff_finalize(lam_ref, g_ref, o_ref, l1, a1, l2, a2, lam_init)


def _attn_prompt(qkv, lam_vecs, subln, slopes, n_heads, hd, lam_init, tq):
    n = qkv.shape[0]
    nb = n // tq
    hw = 2 * hd
    pairs = [(i, j) for i in range(nb) for j in range(i + 1)]
    qi = jnp.asarray([p[0] for p in pairs], jnp.int32)
    kj = jnp.asarray([p[1] for p in pairs], jnp.int32)
    kern = functools.partial(_flash_prompt_kernel, tq=tq, hd=hd, lam_init=lam_init)
    grid_spec = pltpu.PrefetchScalarGridSpec(
        num_scalar_prefetch=3,
        grid=(n_heads, len(pairs)),
        in_specs=[pl.BlockSpec((tq, hw), lambda h, s, qi, kj, sl: (qi[s], h)),
                  pl.BlockSpec((tq, hw), lambda h, s, qi, kj, sl: (kj[s], n_heads + h)),
                  pl.BlockSpec((tq, hw), lambda h, s, qi, kj, sl: (kj[s], 2 * n_heads + h)),
                  pl.BlockSpec((4, hd), lambda h, s, qi, kj, sl: (0, 0)),
                  pl.BlockSpec((1, hw), lambda h, s, qi, kj, sl: (0, 0))],
        out_specs=pl.BlockSpec((tq, hw), lambda h, s, qi, kj, sl: (qi[s], h)),
        scratch_shapes=[pltpu.VMEM((tq, hw), BF16),
                        pltpu.VMEM((tq, 1), F32), pltpu.VMEM((tq, 1), F32), pltpu.VMEM((tq, hw), F32),
                        pltpu.VMEM((tq, 1), F32), pltpu.VMEM((tq, 1), F32), pltpu.VMEM((tq, hw), F32)])
    return pl.pallas_call(
        kern, grid_spec=grid_spec,
        out_shape=jax.ShapeDtypeStruct((n, n_heads * hw), BF16),
        compiler_params=_cparams(2),
        name="attn_prompt",
    )(qi, kj, slopes, qkv, qkv, qkv, lam_vecs, subln.reshape(1, hw))


def _attn_sample_kernel(pt_ref, q_ref, kn_ref, vn_ref, slope_ref, lam_ref, g_ref, *rest,
                        pps, hd, n_heads, page, past_len, lam_init):
    page_refs = rest[:2 * pps]
    o_ref = rest[2 * pps]
    qs, kf_s, vf_s, m_s, l_s, a_s = rest[2 * pps + 1:]
    p = pl.program_id(1)
    rows = q_ref.shape[0]
    prow = page * n_heads

    @pl.when(p == 0)
    def _():
        q = (q_ref[...] * (hd ** -0.5 * LOG2E)).astype(BF16)
        zero = jnp.zeros((rows, hd), BF16)
        qs[0:rows, :] = jnp.concatenate([q[:, :hd], zero], axis=1)
        qs[rows:, :] = jnp.concatenate([zero, q[:, hd:]], axis=1)
        m_s[...] = jnp.full(m_s.shape, NEG_BIG, F32)
        l_s[...] = jnp.zeros(l_s.shape, F32)
        a_s[...] = jnp.zeros(a_s.shape, F32)

    row = lax.broadcasted_iota(jnp.int32, (2 * rows, 1), 0) % rows
    t_row = row // n_heads
    h_row = row % n_heads
    slope = jnp.concatenate([slope_ref[...], slope_ref[...]], axis=0)

    def attend(kf, vf, first_pos, causal):
        cols = kf.shape[0]
        col = lax.broadcasted_iota(jnp.int32, (1, cols), 1)
        tok = col // n_heads
        valid = h_row == col % n_heads
        if causal:
            valid = jnp.logical_and(valid, tok <= t_row)
        bias = slope * (first_pos - past_len + tok).astype(F32)
        sc = lax.dot_general(qs[...], kf, NT_DIMS, preferred_element_type=F32) + bias
        _softmax_update(jnp.where(valid, sc, NEG_BIG), vf, m_s, l_s, a_s)

    for i in range(pps):
        kf_s[i * prow:(i + 1) * prow, :] = page_refs[i][...].reshape(prow, 2 * hd).astype(BF16)
        vf_s[i * prow:(i + 1) * prow, :] = page_refs[pps + i][...].reshape(prow, 2 * hd).astype(BF16)
    attend(kf_s[...], vf_s[...], p * pps * page, False)

    @pl.when(p == pl.num_programs(1) - 1)
    def _():
        attend(kn_ref[...].astype(BF16), vn_ref[...].astype(BF16), past_len, True)
        top, bot = pl.ds(0, rows), pl.ds(rows, rows)
        _diff_finalize(lam_ref, g_ref, o_ref, l_s.at[top], a_s.at[top], l_s.at[bot], a_s.at[bot], lam_init)


def _attn_sample(q, k_new, v_new, cache_k, cache_v, layer, page_table, lam_vecs, subln, slopes,
                 n_heads, hd, lam_init):
    nb, rows, hw = q.shape
    page = cache_k.shape[2]
    n_pages = page_table.shape[1]
    pps = _pick_tile(n_pages, (8, 4, 2, 1))
    kern = functools.partial(_attn_sample_kernel, pps=pps, hd=hd, n_heads=n_heads, page=page,
                             past_len=n_pages * page, lam_init=lam_init)
    slope_rows = jnp.tile(slopes, rows // n_heads).reshape(rows, 1)

    def page_spec(i):
        return pl.BlockSpec((None, None, page, n_heads, hw),
                            lambda b, p, pt: (layer, pt[b * n_pages + p * pps + i], 0, 0, 0))

    row_spec = pl.BlockSpec((None, rows, hw), lambda b, p, pt: (b, 0, 0))
    grid_spec = pltpu.PrefetchScalarGridSpec(
        num_scalar_prefetch=1,
        grid=(nb, n_pages // pps),
        in_specs=[row_spec, row_spec, row_spec,
                  pl.BlockSpec((rows, 1), lambda b, p, pt: (0, 0)),
                  pl.BlockSpec((4, hd), lambda b, p, pt: (0, 0)),
                  pl.BlockSpec((1, hw), lambda b, p, pt: (0, 0))]
                 + [page_spec(i) for i in range(pps)] * 2,
        out_specs=row_spec,
        scratch_shapes=[pltpu.VMEM((2 * rows, hw), BF16),
                        pltpu.VMEM((pps * page * n_heads, hw), BF16),
                        pltpu.VMEM((pps * page * n_heads, hw), BF16),
                        pltpu.VMEM((2 * rows, 1), F32), pltpu.VMEM((2 * rows, 1), F32),
                        pltpu.VMEM((2 * rows, hw), F32)])
    return pl.pallas_call(
        kern, grid_spec=grid_spec,
        out_shape=jax.ShapeDtypeStruct((nb, rows, hw), BF16),
        compiler_params=_cparams(2),
        name="attn_sample",
    )(page_table.reshape(-1), q, k_new, v_new, slope_rows, lam_vecs, subln.reshape(1, hw),
      *([cache_k] * pps), *([cache_v] * pps))


def _s5_discretize(lam_re, lam_im, log_dt, b_re, b_im):
    dt = jnp.exp(log_dt)[:, None]
    mag = jnp.exp(lam_re * dt)
    a_re = mag * jnp.cos(lam_im * dt)
    a_im = mag * jnp.sin(lam_im * dt)
    den = lam_re * lam_re + lam_im * lam_im
    nr = a_re - 1.0
    coef_re = (nr * lam_re + a_im * lam_im) / den
    coef_im = (a_im * lam_re - nr * lam_im) / den
    bb_re = coef_re[..., None] * b_re - coef_im[..., None] * b_im
    bb_im = coef_re[..., None] * b_im + coef_im[..., None] * b_re
    return a_re, a_im, bb_re, bb_im


def _s5_weights(lam_re, lam_im, log_dt, b_re, b_im, c_re, c_im):
    g, p = lam_re.shape
    c = b_re.shape[2]
    gpt = LANES // c
    nt = g // gpt
    a_re, a_im, bb_re, bb_im = _s5_discretize(lam_re, lam_im, log_dt, b_re, b_im)
    eye = jnp.eye(gpt, dtype=F32)

    def in_proj(bb):
        t = bb.reshape(nt, gpt, p, c)
        return jnp.einsum('jgpc,gh->jgchp', t, eye).reshape(nt, gpt * c, gpt * p)

    def out_proj(cm):
        t = cm.reshape(nt, gpt, c, p)
        return jnp.einsum('jgcp,gh->jgphc', t, eye).reshape(nt, gpt * p, gpt * c)

    wb = jnp.concatenate([in_proj(bb_re), in_proj(bb_im)], axis=2).astype(BF16)
    cc = jnp.concatenate([out_proj(c_re), -out_proj(c_im)], axis=1).astype(BF16)
    return a_re.reshape(-1), a_im.reshape(-1), wb, cc


def _s5_prompt_kernel(u_ref, wb_ref, cc_ref, are_ref, aim_ref, d_ref, y_ref, sre_ref, sim_ref,
                      bre, bim, pre, pim, cre, cim, *, tt, seg, last_row, spt):
    i = pl.program_id(0)
    n_slabs = bre.shape[0]
    n_tiles = u_ref.shape[1] // LANES

    @pl.when(i == 0)
    def _():
        cre[...] = jnp.zeros(cre.shape, F32)
        cim[...] = jnp.zeros(cim.shape, F32)
        a_r = are_ref[...]
        a_i = aim_ref[...]
        p_r, p_i = a_r, a_i
        for k in range(seg):
            pre[:, k:k + 1, :] = p_r
            pim[:, k:k + 1, :] = p_i
            p_r, p_i = p_r * a_r - p_i * a_i, p_r * a_i + p_i * a_r

    u = u_ref[...]
    for j in range(n_tiles):
        r = jnp.dot(u[:, j * LANES:(j + 1) * LANES].astype(BF16), wb_ref[j], preferred_element_type=F32)
        for q in range(spt):
            bre[j * spt + q] = r[:, q * LANES:(q + 1) * LANES]
            bim[j * spt + q] = r[:, (spt + q) * LANES:(spt + q + 1) * LANES]

    def slab_body(s, carry):
        a_r = jnp.broadcast_to(are_ref[s], (SUBLANES, LANES))
        a_i = jnp.broadcast_to(aim_ref[s], (SUBLANES, LANES))
        x_r = jnp.zeros((SUBLANES, LANES), F32)
        x_i = jnp.zeros((SUBLANES, LANES), F32)
        local = []
        for k in range(seg):
            b_r = bre[s, pl.ds(k, SUBLANES, stride=seg), :]
            b_i = bim[s, pl.ds(k, SUBLANES, stride=seg), :]
            x_r, x_i = a_r * x_r - a_i * x_i + b_r, a_r * x_i + a_i * x_r + b_i
            local.append((x_r, x_i))
        as_r = pre[s, seg - 1:seg, :]
        as_i = pim[s, seg - 1:seg, :]
        s_r = cre[s]
        s_i = cim[s]
        st_r, st_i = [], []
        for g in range(SUBLANES):
            st_r.append(s_r)
            st_i.append(s_i)
            e_r = x_r[g:g + 1]
            e_i = x_i[g:g + 1]
            s_r, s_i = as_r * s_r - as_i * s_i + e_r, as_r * s_i + as_i * s_r + e_i
        cre[s] = s_r
        cim[s] = s_i
        st_r = jnp.concatenate(st_r, axis=0)
        st_i = jnp.concatenate(st_i, axis=0)
        for k in range(seg):
            p_r = jnp.broadcast_to(pre[s, k:k + 1, :], (SUBLANES, LANES))
            p_i = jnp.broadcast_to(pim[s, k:k + 1, :], (SUBLANES, LANES))
            l_r, l_i = local[k]
            bre[s, pl.ds(k, SUBLANES, stride=seg), :] = l_r + p_r * st_r - p_i * st_i
            bim[s, pl.ds(k, SUBLANES, stride=seg), :] = l_i + p_r * st_i + p_i * st_r
        return carry

    lax.fori_loop(0, n_slabs, slab_body, 0, unroll=2)

    for j in range(n_tiles):
        xcat = jnp.concatenate([bre[j * spt + q] for q in range(spt)]
                               + [bim[j * spt + q] for q in range(spt)], axis=1).astype(BF16)
        sl = slice(j * LANES, (j + 1) * LANES)
        y = jnp.dot(xcat, cc_ref[j], preferred_element_type=F32) + d_ref[:, sl] * u[:, sl]
        y_ref[:, sl] = _gelu(y).astype(y_ref.dtype)

    @pl.when(i == pl.num_programs(0) - 1)
    def _():
        sre_ref[...] = bre[:, last_row:last_row + 1, :]
        sim_ref[...] = bim[:, last_row:last_row + 1, :]


def _s5_prompt(u, a_re, a_im, wb, cc, d, n_valid, tt):
    n, dm = u.shape
    nt, _, two_w = wb.shape
    spt = two_w // 2 // LANES
    n_slabs = nt * spt
    seg = tt // SUBLANES
    nblk = n // tt
    assert (n_valid - 1) // tt == nblk - 1
    kern = functools.partial(_s5_prompt_kernel, tt=tt, seg=seg, last_row=(n_valid - 1) % tt, spt=spt)
    slab = jax.ShapeDtypeStruct((n_slabs, 1, LANES), F32)
    const3 = lambda i: (0, 0, 0)
    y, sre, sim = pl.pallas_call(
        kern, grid=(nblk,),
        in_specs=[pl.BlockSpec((tt, dm), lambda i: (i, 0)),
                  pl.BlockSpec(wb.shape, const3),
                  pl.BlockSpec(cc.shape, const3),
                  pl.BlockSpec((n_slabs, 1, LANES), const3),
                  pl.BlockSpec((n_slabs, 1, LANES), const3),
                  pl.BlockSpec((1, dm), lambda i: (0, 0))],
        out_specs=[pl.BlockSpec((tt, dm), lambda i: (i, 0)),
                   pl.BlockSpec((n_slabs, 1, LANES), const3),
                   pl.BlockSpec((n_slabs, 1, LANES), const3)],
        out_shape=[jax.ShapeDtypeStruct((n, dm), BF16), slab, slab],
        scratch_shapes=[pltpu.VMEM((n_slabs, tt, LANES), F32), pltpu.VMEM((n_slabs, tt, LANES), F32),
                        pltpu.VMEM((n_slabs, seg, LANES), F32), pltpu.VMEM((n_slabs, seg, LANES), F32),
                        pltpu.VMEM((n_slabs, 1, LANES), F32), pltpu.VMEM((n_slabs, 1, LANES), F32)],
        compiler_params=_cparams(1),
        name="s5_prompt",
    )(u, wb, cc, a_re.reshape(n_slabs, 1, LANES), a_im.reshape(n_slabs, 1, LANES), d.reshape(1, dm))
    return y, sre.reshape(-1), sim.reshape(-1)


def _s5_sample_kernel(u_ref, wb_ref, cc_ref, are_ref, aim_ref, d_ref, s0r_ref, s0i_ref,
                      y_ref, sre_ref, sim_ref, bre, bim):
    n_steps = u_ref.shape[0]
    n_tiles = u_ref.shape[2] // LANES
    half = wb_ref.shape[2] // 2
    a_r = are_ref[...]
    a_i = aim_ref[...]
    x_r = s0r_ref[...]
    x_i = s0i_ref[...]
    for t in range(n_steps):
        u = u_ref[t]
        for j in range(n_tiles):
            r = jnp.dot(u[:, j * LANES:(j + 1) * LANES].astype(BF16), wb_ref[j], preferred_element_type=F32)
            bre[:, j * half:(j + 1) * half] = r[:, :half]
            bim[:, j * half:(j + 1) * half] = r[:, half:]
        x_r, x_i = a_r * x_r - a_i * x_i + bre[...], a_r * x_i + a_i * x_r + bim[...]
        for j in range(n_tiles):
            xcat = jnp.concatenate([x_r[:, j * half:(j + 1) * half], x_i[:, j * half:(j + 1) * half]],
                                   axis=1).astype(BF16)
            sl = slice(j * LANES, (j + 1) * LANES)
            y = jnp.dot(xcat, cc_ref[j], preferred_element_type=F32) + d_ref[:, sl] * u[:, sl]
            y_ref[t, :, sl] = _gelu(y).astype(y_ref.dtype)
    sre_ref[...] = x_r
    sim_ref[...] = x_i


def _s5_sample(u, a_re, a_im, wb, cc, d, s0_re, s0_im):
    n_steps, nb, dm = u.shape
    width = s0_re.shape[1]
    tb = _pick_tile(nb, (32, 16, 8))
    const3 = lambda i: (0, 0, 0)
    st = jax.ShapeDtypeStruct((nb, width), F32)
    return pl.pallas_call(
        _s5_sample_kernel, grid=(nb // tb,),
        in_specs=[pl.BlockSpec((n_steps, tb, dm), lambda i: (0, i, 0)),
                  pl.BlockSpec(wb.shape, const3),
                  pl.BlockSpec(cc.shape, const3),
                  pl.BlockSpec((1, width), lambda i: (0, 0)),
                  pl.BlockSpec((1, width), lambda i: (0, 0)),
                  pl.BlockSpec((1, dm), lambda i: (0, 0)),
                  pl.BlockSpec((tb, width), lambda i: (i, 0)),
                  pl.BlockSpec((tb, width), lambda i: (i, 0))],
        out_specs=[pl.BlockSpec((n_steps, tb, dm), lambda i: (0, i, 0)),
                   pl.BlockSpec((tb, width), lambda i: (i, 0)),
                   pl.BlockSpec((tb, width), lambda i: (i, 0))],
        out_shape=[jax.ShapeDtypeStruct((n_steps, nb, dm), BF16), st, st],
        scratch_shapes=[pltpu.VMEM((tb, width), F32), pltpu.VMEM((tb, width), F32)],
        compiler_params=_cparams(1),
        name="s5_sample",
    )(u, wb, cc, a_re.reshape(1, width), a_im.reshape(1, width), d.reshape(1, dm), s0_re, s0_im)


def _topk_rows(s, k):
    n = s.shape[0]
    row = lax.broadcasted_iota(jnp.int32, s.shape, 0)
    vals, idxs = [], []
    for t in range(k):
        m = jnp.max(s, axis=0, keepdims=True)
        idx = jnp.min(jnp.where(s == m, row, n), axis=0, keepdims=True)
        vals.append(m)
        idxs.append(idx)
        if t + 1 < k:
            s = jnp.where(row == idx, -jnp.inf, s)
    return jnp.concatenate(vals, axis=0), jnp.concatenate(idxs, axis=0)


def _candidate_pairs(k):
    return [(i, j) for i in range(k) for j in range(k) if (i + 1) * (j + 1) <= k]


def _peer_router_kernel(x_ref, g_ref, wq_ref, keys_ref, h_ref, gt_ref, i1_ref, i2_ref, q_s,
                        *, n_heads, topk):
    x = x_ref[...]
    hb = (x * _rms_scale(x) * g_ref[...]).astype(BF16)
    h_ref[...] = hb
    q_s[...] = jnp.dot(hb, wq_ref[...], preferred_element_type=F32).astype(BF16)
    tm = x.shape[0]
    n_keys, half = keys_ref.shape[1], keys_ref.shape[2]
    n_chunks = tm // LANES
    pairs = _candidate_pairs(topk)
    counts = [sum(1 for p in pairs if p[0] == i) for i in range(topk)]
    n_cand = _round_up(len(pairs), SUBLANES)

    def body(it, carry):
        hd = it % n_heads
        r0 = pl.multiple_of((it // n_heads) * LANES, LANES)
        q = q_s[pl.ds(r0, LANES), pl.ds(pl.multiple_of(hd * 2 * half, 2 * half), 2 * half)]
        s0 = lax.dot_general(keys_ref[0], q[:, :half], NT_DIMS, preferred_element_type=F32)
        s1 = lax.dot_general(keys_ref[1], q[:, half:], NT_DIMS, preferred_element_type=F32)
        sv0, si0 = _topk_rows(s0, topk)
        sv1, si1 = _topk_rows(s1, topk)
        si0 = si0.astype(F32) * float(n_keys)
        si1 = si1.astype(F32)
        cs, ce = [], []
        for i in range(topk):
            cnt = counts[i]
            cs.append(sv0[i:i + 1] + sv1[0:cnt])
            ce.append(si0[i:i + 1] + si1[0:cnt])
        pad = n_cand - len(pairs)
        if pad:
            cs.append(jnp.full((pad, LANES), -jnp.inf, F32))
            ce.append(jnp.zeros((pad, LANES), F32))
        cand = jnp.concatenate(cs, axis=0)
        cand_e = jnp.concatenate(ce, axis=0)
        row = lax.broadcasted_iota(jnp.int32, cand.shape, 0)
        ts, te = [], []
        for t in range(topk):
            m = jnp.max(cand, axis=0, keepdims=True)
            idx = jnp.min(jnp.where(cand == m, row, n_cand), axis=0, keepdims=True)
            sel = row == idx
            ts.append(m)
            te.append(jnp.sum(jnp.where(sel, cand_e, 0.0), axis=0, keepdims=True))
            if t + 1 < topk:
                cand = jnp.where(sel, -jnp.inf, cand)
        ts = jnp.concatenate(ts, axis=0)
        te = jnp.concatenate(te, axis=0)
        e = jnp.exp(ts - ts[0:1])
        gate = e / jnp.sum(e, axis=0, keepdims=True)
        t1 = jnp.floor(te * (1.0 / n_keys))
        rows = pl.ds(pl.multiple_of(hd * topk, topk), topk)
        cols = pl.ds(r0, LANES)
        gt_ref[rows, cols] = gate
        i1_ref[rows, cols] = t1
        i2_ref[rows, cols] = te - t1 * float(n_keys)
        return carry

    lax.fori_loop(0, n_heads * n_chunks, body, 0, unroll=4)


def _peer_router(x, g, w_q, sub_keys):
    n, d = x.shape
    n_keys, half = sub_keys.shape[1], sub_keys.shape[2]
    assert n_keys == LANES and half == LANES
    n_heads = w_q.shape[1] // (2 * half)
    hk = n_heads * PEER_TOPK
    tm = _pick_tile(n, (640, 512, 256, 128))
    kern = functools.partial(_peer_router_kernel, n_heads=n_heads, topk=PEER_TOPK)
    tr = jax.ShapeDtypeStruct((hk, n), F32)
    return pl.pallas_call(
        kern, grid=(n // tm,),
        in_specs=[pl.BlockSpec((tm, d), lambda i: (i, 0)),
                  pl.BlockSpec((1, d), lambda i: (0, 0)),
                  pl.BlockSpec(w_q.shape, lambda i: (0, 0)),
                  pl.BlockSpec(sub_keys.shape, lambda i: (0, 0, 0))],
        out_specs=[pl.BlockSpec((tm, d), lambda i: (i, 0)),
                   pl.BlockSpec((hk, tm), lambda i: (0, i)),
                   pl.BlockSpec((hk, tm), lambda i: (0, i)),
                   pl.BlockSpec((hk, tm), lambda i: (0, i))],
        out_shape=[jax.ShapeDtypeStruct((n, d), BF16), tr, tr, tr],
        scratch_shapes=[pltpu.VMEM((tm, w_q.shape[1]), BF16)],
        compiler_params=_cparams(1),
        name="peer_router",
    )(x, g.reshape(1, d), w_q, sub_keys)


def _peer_expand_kernel(gt_ref, i1_ref, i2_ref, *rest, pitch, cast_tables):
    if cast_tables:
        u_ref, v_ref, w_ref, ub_ref, vb_ref, g_s, a_s, b_s, scr = rest
        ub_ref[...] = u_ref[...].astype(BF16)
        vb_ref[...] = v_ref[...].astype(BF16)
    else:
        w_ref, g_s, a_s, b_s, scr = rest
    g_s[...] = gt_ref[...].T
    a_s[...] = i1_ref[...].T
    b_s[...] = i2_ref[...].T
    tn, hk = g_s.shape
    n_keys = w_ref.shape[1] // LANES
    sub = lax.broadcasted_iota(jnp.int32, (n_keys, hk), 0).astype(F32)

    def body(n, carry):
        at = jnp.where(a_s[pl.ds(n, 1), :] == sub, g_s[pl.ds(n, 1), :], 0.0).astype(BF16)
        bt = jnp.where(b_s[pl.ds(n, 1), :] == sub, 1.0, 0.0).astype(BF16)
        scr[pl.ds(pl.multiple_of(n * pitch, SUBLANES), n_keys), :] = lax.dot_general(
            at, bt, NT_DIMS, preferred_element_type=F32)
        return carry

    lax.fori_loop(0, tn, body, 0, unroll=16)
    for a in range(n_keys):
        w_ref[:, a * LANES:(a + 1) * LANES] = scr[pl.ds(a, tn, stride=pitch), :].astype(w_ref.dtype)


def _peer_expand(gt, i1, i2, n_keys, tables=None):
    hk, n = gt.shape
    assert n_keys == LANES and hk == LANES
    tn = LANES
    steps = n // tn
    pitch = n_keys + SUBLANES
    kern = functools.partial(_peer_expand_kernel, pitch=pitch, cast_tables=tables is not None)
    spec = pl.BlockSpec((hk, tn), lambda i: (0, i))
    in_specs = [spec, spec, spec]
    out_specs = [pl.BlockSpec((tn, n_keys * n_keys), lambda i: (i, 0))]
    out_shape = [jax.ShapeDtypeStruct((n, n_keys * n_keys), BF16)]
    args = [gt, i1, i2]
    if tables is not None:
        layer, u_all, v_all = tables
        _, n_exp, d = u_all.shape
        nb = 1 << (steps.bit_length() - 1)
        assert n_exp % nb == 0
        tr = n_exp // nb
        in_specs += [pl.BlockSpec((None, tr, d), lambda i: (layer, jnp.minimum(i, nb - 1), 0))] * 2
        out_specs += [pl.BlockSpec((tr, d), lambda i: (jnp.minimum(i, nb - 1), 0))] * 2
        out_shape += [jax.ShapeDtypeStruct((n_exp, d), BF16)] * 2
        args += [u_all, v_all]
    return pl.pallas_call(
        kern, grid=(steps,),
        in_specs=in_specs, out_specs=out_specs, out_shape=out_shape,
        scratch_shapes=[pltpu.VMEM((tn, hk), F32), pltpu.VMEM((tn, hk), F32), pltpu.VMEM((tn, hk), F32),
                        pltpu.VMEM((tn * pitch, LANES), F32)],
        compiler_params=_cparams(1),
        name="peer_expand",
    )(*args)


def _peer_dense_kernel(h_ref, u_ref, v_ref, w_ref, r_ref, o_ref):
    @pl.when(pl.program_id(1) == 0)
    def _():
        o_ref[...] = r_ref[...]

    act = lax.dot_general(h_ref[...], u_ref[...], NT_DIMS, preferred_element_type=F32)
    p = (w_ref[...].astype(F32) * _gelu(act)).astype(BF16)
    o_ref[...] += jnp.dot(p, v_ref[...], preferred_element_type=F32)


def _peer_dense(h, u_tab, v_tab, w, res):
    n, d = h.shape
    n_exp = u_tab.shape[0]
    tm = _pick_tile(n, (640, 512, 256, 128))
    te = 1024
    once = pl.Buffered(1)
    return pl.pallas_call(
        _peer_dense_kernel, grid=(n // tm, n_exp // te),
        in_specs=[pl.BlockSpec((tm, d), lambda i, e: (i, 0), pipeline_mode=once),
                  pl.BlockSpec((te, d), lambda i, e: (e, 0)),
                  pl.BlockSpec((te, d), lambda i, e: (e, 0)),
                  pl.BlockSpec((tm, te), lambda i, e: (i, e)),
                  pl.BlockSpec((tm, d), lambda i, e: (i, 0), pipeline_mode=once)],
        out_specs=pl.BlockSpec((tm, d), lambda i, e: (i, 0)),
        out_shape=jax.ShapeDtypeStruct((n, d), F32),
        compiler_params=_cparams(2),
        name="peer_dense",
    )(h, u_tab, v_tab, w, res)


def _peer_ffn(x, g, w_q, sub_keys, tables, tables_are_f32):
    h, gt, i1, i2 = _peer_router(x, g, w_q, sub_keys)
    if tables_are_f32:
        w, u_tab, v_tab = _peer_expand(gt, i1, i2, sub_keys.shape[1], tables)
    else:
        (w,) = _peer_expand(gt, i1, i2, sub_keys.shape[1])
        u_tab, v_tab = tables
    return _peer_dense(h, u_tab, v_tab, w, x), (u_tab, v_tab)


def kernel(x_prompt, x_sample, cache_k, cache_v, state_ssm_re, state_ssm_im, page_table, meta_tokens, norm_mix, norm_ffn, norm_final, attn_w_qkv, attn_lambda, attn_subln, attn_w_o, ssm_w_in, ssm_lambda_re, ssm_lambda_im, ssm_log_dt, ssm_b_re, ssm_b_im, ssm_c_re, ssm_c_im, ssm_d, ssm_w_glu, peer_w_q, peer_sub_keys, peer_u, peer_v):
    batch, seq, d_model = x_prompt.shape
    assert batch == 1
    dec_batch, dec_seq, _ = x_sample.shape
    n_meta = meta_tokens.shape[0]
    depth = norm_mix.shape[0]
    n_heads, hw = cache_k.shape[3], cache_k.shape[4]
    hd = hw // 2
    n_groups, state_dim = ssm_lambda_re.shape[1], ssm_lambda_re.shape[2]
    t_tot = n_meta + seq
    tq = 640
    tt = 160
    n_p = _round_up(t_tot, math.lcm(tq, tt))
    n_s = dec_batch * dec_seq
    slopes = jnp.asarray(2.0 ** (-8.0 * np.arange(1, n_heads + 1) / n_heads) * LOG2E, dtype=F32)

    xp = jnp.concatenate([meta_tokens.astype(F32), x_prompt[0],
                          jnp.zeros((n_p - t_tot, d_model), F32)], axis=0)
    xs = x_sample.reshape(n_s, d_model)

    outs = {k: [] for k in ("kp", "vp", "ks", "vs", "srp", "sip", "srs", "sis")}
    for i in range(depth):
        j = i // 2
        if i % 2 == 0:
            lam_init = _lambda_init(i)
            w_qkv = attn_w_qkv[j].astype(BF16)
            w_o = attn_w_o[j].astype(BF16)
            qk_w = n_heads * hw
            qkv_p = _norm_matmul(xp, norm_mix[i], w_qkv)
            qkv_s = _norm_matmul(xs, norm_mix[i], w_qkv)
            o_p = _attn_prompt(qkv_p, attn_lambda[j], attn_subln[j], slopes, n_heads, hd, lam_init, tq)

            def rows(a):
                return a.reshape(dec_batch, dec_seq * n_heads, hw)

            o_s = _attn_sample(rows(qkv_s[:, :qk_w]), rows(qkv_s[:, qk_w:2 * qk_w]), rows(qkv_s[:, 2 * qk_w:]),
                               cache_k, cache_v, j, page_table, attn_lambda[j], attn_subln[j], slopes,
                               n_heads, hd, lam_init)
            xp = _matmul_residual(o_p, w_o, xp)
            xs = _matmul_residual(o_s.reshape(n_s, qk_w), w_o, xs)
            outs["kp"].append(qkv_p[:t_tot, qk_w:2 * qk_w].reshape(batch, t_tot, n_heads, hw))
            outs["vp"].append(qkv_p[:t_tot, 2 * qk_w:].reshape(batch, t_tot, n_heads, hw))
            outs["ks"].append(qkv_s[:, qk_w:2 * qk_w].reshape(dec_batch, dec_seq, n_heads, hw))
            outs["vs"].append(qkv_s[:, 2 * qk_w:].reshape(dec_batch, dec_seq, n_heads, hw))
        else:
            w_in = ssm_w_in[j].astype(BF16)
            w_glu = ssm_w_glu[j].astype(BF16)
            a_re, a_im, wb, cc = _s5_weights(ssm_lambda_re[j], ssm_lambda_im[j], ssm_log_dt[j],
                                             ssm_b_re[j], ssm_b_im[j], ssm_c_re[j], ssm_c_im[j])
            d_vec = ssm_d[j].reshape(-1)
            u_p = _norm_matmul(xp, norm_mix[i], w_in)
            y_p, sr_p, si_p = _s5_prompt(u_p, a_re, a_im, wb, cc, d_vec, t_tot, tt)
            xp = _glu_residual(y_p, w_glu, xp)
            u_s = _norm_matmul(xs, norm_mix[i], w_in)
            u_s = u_s.reshape(dec_batch, dec_seq, d_model).transpose(1, 0, 2)
            y_s, sr_s, si_s = _s5_sample(u_s, a_re, a_im, wb, cc, d_vec,
                                         state_ssm_re[j].reshape(dec_batch, -1),
                                         state_ssm_im[j].reshape(dec_batch, -1))
            xs = _glu_residual(y_s.transpose(1, 0, 2).reshape(n_s, d_model), w_glu, xs)
            outs["srp"].append(sr_p.reshape(batch, n_groups, state_dim))
            outs["sip"].append(si_p.reshape(batch, n_groups, state_dim))
            outs["srs"].append(sr_s.reshape(dec_batch, n_groups, state_dim))
            outs["sis"].append(si_s.reshape(dec_batch, n_groups, state_dim))
        w_q = peer_w_q[i].astype(BF16)
        keys = peer_sub_keys[i].astype(BF16)
        xp, tables = _peer_ffn(xp, norm_ffn[i], w_q, keys, (i, peer_u, peer_v), True)
        xs, _ = _peer_ffn(xs, norm_ffn[i], w_q, keys, tables, False)

    y_prompt = _final_norm(xp, norm_final)[n_meta:t_tot].reshape(batch, seq, d_model)
    y_sample = _final_norm(xs, norm_final).reshape(dec_batch, dec_seq, d_model)
    return (y_prompt, y_sample,
            jnp.stack(outs["kp"]), jnp.stack(outs["vp"]), jnp.stack(outs["srp"]), jnp.stack(outs["sip"]),
            jnp.stack(outs["ks"]), jnp.stack(outs["vs"]), jnp.stack(outs["srs"]), jnp.stack(outs["sis"]))
```

```python
import functools
import math

import numpy as np
import jax
import jax.numpy as jnp
from jax import lax
from jax.experimental import pallas as pl
from jax.experimental.pallas import tpu as pltpu

F32 = jnp.float32
BF16 = jnp.bfloat16
EPS = 1e-6
PEER_TOPK = 16
LANES = 128
SUBLANES = 8
NEG_BIG = -1e30
LOG2E = math.log2(math.e)
VMEM_LIMIT = 56 * 1024 * 1024
NT_DIMS = (((1,), (1,)), ((), ()))


def _cparams(n_axes):
    return pltpu.CompilerParams(dimension_semantics=("arbitrary",) * n_axes,
                                vmem_limit_bytes=VMEM_LIMIT)


def _round_up(x, m):
    return (x + m - 1) // m * m


def _pick_tile(n, candidates):
    for c in candidates:
        if n % c == 0:
            return c
    return n


def _gelu(x):
    return 0.5 * x * (1.0 + lax.erf(x * (1.0 / math.sqrt(2.0))))


def _rms_scale(x):
    return lax.rsqrt(jnp.mean(x * x, axis=-1, keepdims=True) + EPS)


def _lambda_init(layer):
    return 0.8 - 0.6 * math.exp(-0.3 * layer)


def _norm_mm_kernel(x_ref, g_ref, w_ref, o_ref, h_ref):
    @pl.when(pl.program_id(1) == 0)
    def _():
        x = x_ref[...]
        h_ref[...] = (x * _rms_scale(x) * g_ref[...]).astype(BF16)

    o_ref[...] = jnp.dot(h_ref[...], w_ref[...], preferred_element_type=F32)


def _norm_matmul(x, g, w):
    n, d = x.shape
    m = w.shape[1]
    tm = _pick_tile(n, (640, 512, 256, 128))
    tn = _pick_tile(m, (1024, 512, 256, 128))
    return pl.pallas_call(
        _norm_mm_kernel,
        grid=(n // tm, m // tn),
        in_specs=[pl.BlockSpec((tm, d), lambda i, j: (i, 0)),
                  pl.BlockSpec((1, d), lambda i, j: (0, 0)),
                  pl.BlockSpec((d, tn), lambda i, j: (0, j))],
        out_specs=pl.BlockSpec((tm, tn), lambda i, j: (i, j)),
        out_shape=jax.ShapeDtypeStruct((n, m), F32),
        scratch_shapes=[pltpu.VMEM((tm, d), BF16)],
        compiler_params=_cparams(2),
        name="norm_matmul",
    )(x, g.reshape(1, d), w)


def _qkv_kernel(x_ref, g_ref, w_ref, o_ref, ko_ref, vo_ref, h_ref, *, nq):
    j = pl.program_id(1)

    @pl.when(j == 0)
    def _():
        x = x_ref[...]
        h_ref[...] = (x * _rms_scale(x) * g_ref[...]).astype(BF16)

    acc = jnp.dot(h_ref[...], w_ref[...], preferred_element_type=F32)
    o_ref[...] = acc

    @pl.when(jnp.logical_and(j >= nq, j < 2 * nq))
    def _():
        ko_ref[...] = acc

    @pl.when(j >= 2 * nq)
    def _():
        vo_ref[...] = acc


def _qkv_matmul(x, g, w, n_valid):
    n, d = x.shape
    width = w.shape[1] // 3
    tm = _pick_tile(n, (640, 512, 256, 128))
    tn = _pick_tile(width, (1024, 512, 256, 128))
    nq = width // tn
    part = jax.ShapeDtypeStruct((n_valid, width), F32)
    return pl.pallas_call(
        functools.partial(_qkv_kernel, nq=nq),
        grid=(n // tm, 3 * nq),
        in_specs=[pl.BlockSpec((tm, d), lambda i, j: (i, 0)),
                  pl.BlockSpec((1, d), lambda i, j: (0, 0)),
                  pl.BlockSpec((d, tn), lambda i, j: (0, j))],
        out_specs=[pl.BlockSpec((tm, tn), lambda i, j: (i, j)),
                   pl.BlockSpec((tm, tn), lambda i, j: (i, jnp.clip(j - nq, 0, nq - 1))),
                   pl.BlockSpec((tm, tn), lambda i, j: (i, jnp.clip(j - 2 * nq, 0, nq - 1)))],
        out_shape=[jax.ShapeDtypeStruct((n, 3 * width), F32), part, part],
        scratch_shapes=[pltpu.VMEM((tm, d), BF16)],
        compiler_params=_cparams(2),
        name="qkv_matmul",
    )(x, g.reshape(1, d), w)


def _mm_res_kernel(a_ref, w_ref, r_ref, o_ref):
    o_ref[...] = r_ref[...] + jnp.dot(a_ref[...], w_ref[...], preferred_element_type=F32)


def _matmul_residual(a, w, res):
    n, k = a.shape
    m = w.shape[1]
    tm = _pick_tile(n, (640, 512, 256, 128))
    tn = _pick_tile(m, (1024, 512, 256, 128))
    return pl.pallas_call(
        _mm_res_kernel,
        grid=(n // tm, m // tn),
        in_specs=[pl.BlockSpec((tm, k), lambda i, j: (i, 0)),
                  pl.BlockSpec((k, tn), lambda i, j: (0, j)),
                  pl.BlockSpec((tm, tn), lambda i, j: (i, j))],
        out_specs=pl.BlockSpec((tm, tn), lambda i, j: (i, j)),
        out_shape=jax.ShapeDtypeStruct((n, m), F32),
        compiler_params=_cparams(2),
        name="matmul_residual",
    )(a, w, res)


def _glu_res_kernel(a_ref, wv_ref, wg_ref, r_ref, o_ref):
    a = a_ref[...]
    zv = jnp.dot(a, wv_ref[...], preferred_element_type=F32)
    zg = jnp.dot(a, wg_ref[...], preferred_element_type=F32)
    o_ref[...] = r_ref[...] + zv * jax.nn.sigmoid(zg)


def _glu_residual(a, w, res):
    n, k = a.shape
    m = w.shape[1] // 2
    tm = _pick_tile(n, (640, 512, 256, 128))
    tn = _pick_tile(m, (512, 256, 128))
    nj = m // tn
    return pl.pallas_call(
        _glu_res_kernel,
        grid=(n // tm, nj),
        in_specs=[pl.BlockSpec((tm, k), lambda i, j: (i, 0)),
                  pl.BlockSpec((k, tn), lambda i, j: (0, j)),
                  pl.BlockSpec((k, tn), lambda i, j: (0, j + nj)),
                  pl.BlockSpec((tm, tn), lambda i, j: (i, j))],
        out_specs=pl.BlockSpec((tm, tn), lambda i, j: (i, j)),
        out_shape=jax.ShapeDtypeStruct((n, m), F32),
        compiler_params=_cparams(2),
        name="glu_residual",
    )(a, w, w, res)


def _final_norm_kernel(x_ref, g_ref, o_ref):
    x = x_ref[...]
    o_ref[...] = x * _rms_scale(x) * g_ref[...]


def _final_norm(x, g):
    n, d = x.shape
    tm = _pick_tile(n, (640, 512, 256, 128))
    return pl.pallas_call(
        _final_norm_kernel,
        grid=(n // tm,),
        in_specs=[pl.BlockSpec((tm, d), lambda i: (i, 0)),
                  pl.BlockSpec((1, d), lambda i: (0, 0))],
        out_specs=pl.BlockSpec((tm, d), lambda i: (i, 0)),
        out_shape=jax.ShapeDtypeStruct((n, d), F32),
        compiler_params=_cparams(1),
        name="final_norm",
    )(x, g.reshape(1, d))


def _diff_lambda(lam_ref, lam_init):
    lv = lam_ref[...]
    s1 = jnp.sum(lv[0:1] * lv[1:2], axis=1, keepdims=True)
    s2 = jnp.sum(lv[2:3] * lv[3:4], axis=1, keepdims=True)
    return jnp.exp(s1) - jnp.exp(s2) + lam_init


def _softmax_update(sc, v, m_ref, l_ref, a_ref):
    m_old = m_ref[...]
    m_new = jnp.maximum(m_old, jnp.max(sc, axis=1, keepdims=True))
    alpha = jnp.exp2(m_old - m_new)
    p = jnp.exp2(sc - m_new)
    l_ref[...] = alpha * l_ref[...] + jnp.sum(p, axis=1, keepdims=True)
    a_ref[...] = alpha * a_ref[...] + jnp.dot(p.astype(BF16), v, preferred_element_type=F32)
    m_ref[...] = m_new


def _diff_finalize(lam_ref, g_ref, o_ref, l1, a1, l2, a2, lam_init):
    lam = _diff_lambda(lam_ref, lam_init)
    o = a1[...] / l1[...] - lam * (a2[...] / l2[...])
    o_ref[...] = ((o * _rms_scale(o) * g_ref[...]) * (1.0 - lam_init)).astype(o_ref.dtype)


def _init_softmax_state(m1, l1, a1, m2, l2, a2):
    for m, l, a in ((m1, l1, a1), (m2, l2, a2)):
        m[...] = jnp.full(m.shape, NEG_BIG, F32)
        l[...] = jnp.zeros(l.shape, F32)
        a[...] = jnp.zeros(a.shape, F32)


def _flash_prompt_kernel(qi_ref, kj_ref, slope_ref, q_ref, k_ref, v_ref, lam_ref, g_ref, o_ref,
                         qs, m1, l1, a1, m2, l2, a2, *, tq, hd, lam_init):
    h = pl.program_id(0)
    s = pl.program_id(1)
    qi = qi_ref[s]
    kj = kj_ref[s]

    @pl.when(kj == 0)
    def _():
        qs[...] = (q_ref[...] * (hd ** -0.5 * LOG2E)).astype(BF16)
        _init_softmax_state(m1, l1, a1, m2, l2, a2)

    def update(masked):
        k = k_ref[...].astype(BF16)
        v = v_ref[...].astype(BF16)
        col = lax.broadcasted_iota(jnp.int32, (1, tq), 1)
        bias = slope_ref[h] * ((kj - qi) * tq + col).astype(F32)
        if masked:
            keep = (lax.broadcasted_iota(jnp.int32, (tq, tq), 1)
                    <= lax.broadcasted_iota(jnp.int32, (tq, tq), 0))
        for lo, m_ref, l_ref, a_ref in ((0, m1, l1, a1), (hd, m2, l2, a2)):
            sc = lax.dot_general(qs[:, lo:lo + hd], k[:, lo:lo + hd], NT_DIMS,
                                 preferred_element_type=F32) + bias
            if masked:
                sc = jnp.where(keep, sc, NEG_BIG)
            _softmax_update(sc, v, m_ref, l_ref, a_ref)

    @pl.when(kj < qi)
    def _():
        update(False)

    @pl.when(kj == qi)
    def _():
        update(True)
        _diff_finalize(lam_ref, g_ref, o_ref, l1, a1, l2, a2, lam_init)


def _attn_prompt(qkv, lam_vecs, subln, slopes, n_heads, hd, lam_init, tq):
    n = qkv.shape[0]
    nb = n // tq
    hw = 2 * hd
    pairs = [(i, j) for i in range(nb) for j in range(i + 1)]
    qi = jnp.asarray([p[0] for p in pairs], jnp.int32)
    kj = jnp.asarray([p[1] for p in pairs], jnp.int32)
    kern = functools.partial(_flash_prompt_kernel, tq=tq, hd=hd, lam_init=lam_init)
    grid_spec = pltpu.PrefetchScalarGridSpec(
        num_scalar_prefetch=3,
        grid=(n_heads, len(pairs)),
        in_specs=[pl.BlockSpec((tq, hw), lambda h, s, qi, kj, sl: (qi[s], h)),
                  pl.BlockSpec((tq, hw), lambda h, s, qi, kj, sl: (kj[s], n_heads + h)),
                  pl.BlockSpec((tq, hw), lambda h, s, qi, kj, sl: (kj[s], 2 * n_heads + h)),
                  pl.BlockSpec((4, hd), lambda h, s, qi, kj, sl: (0, 0)),
                  pl.BlockSpec((1, hw), lambda h, s, qi, kj, sl: (0, 0))],
        out_specs=pl.BlockSpec((tq, hw), lambda h, s, qi, kj, sl: (qi[s], h)),
        scratch_shapes=[pltpu.VMEM((tq, hw), BF16),
                        pltpu.VMEM((tq, 1), F32), pltpu.VMEM((tq, 1), F32), pltpu.VMEM((tq, hw), F32),
                        pltpu.VMEM((tq, 1), F32), pltpu.VMEM((tq, 1), F32), pltpu.VMEM((tq, hw), F32)])
    return pl.pallas_call(
        kern, grid_spec=grid_spec,
        out_shape=jax.ShapeDtypeStruct((n, n_heads * hw), BF16),
        compiler_params=_cparams(2),
        name="attn_prompt",
    )(qi, kj, slopes, qkv, qkv, qkv, lam_vecs, subln.reshape(1, hw))


def _attn_sample_kernel(pt_ref, q_ref, kn_ref, vn_ref, slope_ref, lam_ref, g_ref, *rest,
                        pps, hd, n_heads, page, past_len, lam_init):
    page_refs = rest[:2 * pps]
    o_ref = rest[2 * pps]
    qs, kf_s, vf_s, m_s, l_s, a_s = rest[2 * pps + 1:]
    p = pl.program_id(1)
    rows = q_ref.shape[0]
    prow = page * n_heads

    @pl.when(p == 0)
    def _():
        q = (q_ref[...] * (hd ** -0.5 * LOG2E)).astype(BF16)
        zero = jnp.zeros((rows, hd), BF16)
        qs[0:rows, :] = jnp.concatenate([q[:, :hd], zero], axis=1)
        qs[rows:, :] = jnp.concatenate([zero, q[:, hd:]], axis=1)
        m_s[...] = jnp.full(m_s.shape, NEG_BIG, F32)
        l_s[...] = jnp.zeros(l_s.shape, F32)
        a_s[...] = jnp.zeros(a_s.shape, F32)

    row = lax.broadcasted_iota(jnp.int32, (2 * rows, 1), 0) % rows
    t_row = row // n_heads
    h_row = row % n_heads
    slope = jnp.concatenate([slope_ref[...], slope_ref[...]], axis=0)

    def attend(kf, vf, first_pos, causal):
        cols = kf.shape[0]
        col = lax.broadcasted_iota(jnp.int32, (1, cols), 1)
        tok = col // n_heads
        valid = h_row == col % n_heads
        if causal:
            valid = jnp.logical_and(valid, tok <= t_row)
        bias = slope * (first_pos - past_len + tok).astype(F32)
        sc = lax.dot_general(qs[...], kf, NT_DIMS, preferred_element_type=F32) + bias
        _softmax_update(jnp.where(valid, sc, NEG_BIG), vf, m_s, l_s, a_s)

    for i in range(pps):
        kf_s[i * prow:(i + 1) * prow, :] = page_refs[i][...].reshape(prow, 2 * hd).astype(BF16)
        vf_s[i * prow:(i + 1) * prow, :] = page_refs[pps + i][...].reshape(prow, 2 * hd).astype(BF16)
    attend(kf_s[...], vf_s[...], p * pps * page, False)

    @pl.when(p == pl.num_programs(1) - 1)
    def _():
        attend(kn_ref[...].astype(BF16), vn_ref[...].astype(BF16), past_len, True)
        top, bot = pl.ds(0, rows), pl.ds(rows, rows)
        _diff_finalize(lam_ref, g_ref, o_ref, l_s.at[top], a_s.at[top], l_s.at[bot], a_s.at[bot], lam_init)


def _attn_sample(q, k_new, v_new, cache_k, cache_v, layer, page_table, lam_vecs, subln, slopes,
                 n_heads, hd, lam_init):
    nb, rows, hw = q.shape
    page = cache_k.shape[2]
    n_pages = page_table.shape[1]
    pps = _pick_tile(n_pages, (8, 4, 2, 1))
    kern = functools.partial(_attn_sample_kernel, pps=pps, hd=hd, n_heads=n_heads, page=page,
                             past_len=n_pages * page, lam_init=lam_init)
    slope_rows = jnp.tile(slopes, rows // n_heads).reshape(rows, 1)

    def page_spec(i):
        return pl.BlockSpec((None, None, page, n_heads, hw),
                            lambda b, p, pt: (layer, pt[b * n_pages + p * pps + i], 0, 0, 0))

    row_spec = pl.BlockSpec((None, rows, hw), lambda b, p, pt: (b, 0, 0))
    grid_spec = pltpu.PrefetchScalarGridSpec(
        num_scalar_prefetch=1,
        grid=(nb, n_pages // pps),
        in_specs=[row_spec, row_spec, row_spec,
                  pl.BlockSpec((rows, 1), lambda b, p, pt: (0, 0)),
                  pl.BlockSpec((4, hd), lambda b, p, pt: (0, 0)),
                  pl.BlockSpec((1, hw), lambda b, p, pt: (0, 0))]
                 + [page_spec(i) for i in range(pps)] * 2,
        out_specs=row_spec,
        scratch_shapes=[pltpu.VMEM((2 * rows, hw), BF16),
                        pltpu.VMEM((pps * page * n_heads, hw), BF16),
                        pltpu.VMEM((pps * page * n_heads, hw), BF16),
                        pltpu.VMEM((2 * rows, 1), F32), pltpu.VMEM((2 * rows, 1), F32),
                        pltpu.VMEM((2 * rows, hw), F32)])
    return pl.pallas_call(
        kern, grid_spec=grid_spec,
        out_shape=jax.ShapeDtypeStruct((nb, rows, hw), BF16),
        compiler_params=_cparams(2),
        name="attn_sample",
    )(page_table.reshape(-1), q, k_new, v_new, slope_rows, lam_vecs, subln.reshape(1, hw),
      *([cache_k] * pps), *([cache_v] * pps))


def _s5_discretize(lam_re, lam_im, log_dt, b_re, b_im):
    dt = jnp.exp(log_dt)[:, None]
    mag = jnp.exp(lam_re * dt)
    a_re = mag * jnp.cos(lam_im * dt)
    a_im = mag * jnp.sin(lam_im * dt)
    den = lam_re * lam_re + lam_im * lam_im
    nr = a_re - 1.0
    coef_re = (nr * lam_re + a_im * lam_im) / den
    coef_im = (a_im * lam_re - nr * lam_im) / den
    bb_re = coef_re[..., None] * b_re - coef_im[..., None] * b_im
    bb_im = coef_re[..., None] * b_im + coef_im[..., None] * b_re
    return a_re, a_im, bb_re, bb_im


def _s5_weights(lam_re, lam_im, log_dt, b_re, b_im, c_re, c_im):
    g, p = lam_re.shape
    c = b_re.shape[2]
    gpt = LANES // c
    nt = g // gpt
    a_re, a_im, bb_re, bb_im = _s5_discretize(lam_re, lam_im, log_dt, b_re, b_im)
    eye = jnp.eye(gpt, dtype=F32)

    def in_proj(bb):
        t = bb.reshape(nt, gpt, p, c)
        return jnp.einsum('jgpc,gh->jgchp', t, eye).reshape(nt, gpt * c, gpt * p)

    def out_proj(cm):
        t = cm.reshape(nt, gpt, c, p)
        return jnp.einsum('jgcp,gh->jgphc', t, eye).reshape(nt, gpt * p, gpt * c)

    wb = jnp.concatenate([in_proj(bb_re), in_proj(bb_im)], axis=2).astype(BF16)
    cc = jnp.concatenate([out_proj(c_re), -out_proj(c_im)], axis=1).astype(BF16)
    return a_re.reshape(-1), a_im.reshape(-1), wb, cc


def _s5_prompt_kernel(u_ref, wb_ref, cc_ref, are_ref, aim_ref, d_ref, y_ref, sre_ref, sim_ref,
                      bre, bim, pre, pim, cre, cim, *, tt, seg, last_row, spt):
    i = pl.program_id(0)
    n_slabs = bre.shape[0]
    n_tiles = u_ref.shape[1] // LANES

    @pl.when(i == 0)
    def _():
        cre[...] = jnp.zeros(cre.shape, F32)
        cim[...] = jnp.zeros(cim.shape, F32)
        a_r = are_ref[...]
        a_i = aim_ref[...]
        p_r, p_i = a_r, a_i
        for k in range(seg):
            pre[:, k:k + 1, :] = p_r
            pim[:, k:k + 1, :] = p_i
            p_r, p_i = p_r * a_r - p_i * a_i, p_r * a_i + p_i * a_r

    u = u_ref[...]
    for j in range(n_tiles):
        r = jnp.dot(u[:, j * LANES:(j + 1) * LANES].astype(BF16), wb_ref[j], preferred_element_type=F32)
        for q in range(spt):
            bre[j * spt + q] = r[:, q * LANES:(q + 1) * LANES]
            bim[j * spt + q] = r[:, (spt + q) * LANES:(spt + q + 1) * LANES]

    def slab_body(s, carry):
        a_r = jnp.broadcast_to(are_ref[s], (SUBLANES, LANES))
        a_i = jnp.broadcast_to(aim_ref[s], (SUBLANES, LANES))
        x_r = jnp.zeros((SUBLANES, LANES), F32)
        x_i = jnp.zeros((SUBLANES, LANES), F32)
        local = []
        for k in range(seg):
            b_r = bre[s, pl.ds(k, SUBLANES, stride=seg), :]
            b_i = bim[s, pl.ds(k, SUBLANES, stride=seg), :]
            x_r, x_i = a_r * x_r - a_i * x_i + b_r, a_r * x_i + a_i * x_r + b_i
            local.append((x_r, x_i))
        as_r = pre[s, seg - 1:seg, :]
        as_i = pim[s, seg - 1:seg, :]
        s_r = cre[s]
        s_i = cim[s]
        st_r, st_i = [], []
        for g in range(SUBLANES):
            st_r.append(s_r)
            st_i.append(s_i)
            e_r = x_r[g:g + 1]
            e_i = x_i[g:g + 1]
            s_r, s_i = as_r * s_r - as_i * s_i + e_r, as_r * s_i + as_i * s_r + e_i
        cre[s] = s_r
        cim[s] = s_i
        st_r = jnp.concatenate(st_r, axis=0)
        st_i = jnp.concatenate(st_i, axis=0)
        for k in range(seg):
            p_r = jnp.broadcast_to(pre[s, k:k + 1, :], (SUBLANES, LANES))
            p_i = jnp.broadcast_to(pim[s, k:k + 1, :], (SUBLANES, LANES))
            l_r, l_i = local[k]
            bre[s, pl.ds(k, SUBLANES, stride=seg), :] = l_r + p_r * st_r - p_i * st_i
            bim[s, pl.ds(k, SUBLANES, stride=seg), :] = l_i + p_r * st_i + p_i * st_r
        return carry

    lax.fori_loop(0, n_slabs, slab_body, 0, unroll=2)

    for j in range(n_tiles):
        xcat = jnp.concatenate([bre[j * spt + q] for q in range(spt)]
                               + [bim[j * spt + q] for q in range(spt)], axis=1).astype(BF16)
        sl = slice(j * LANES, (j + 1) * LANES)
        y = jnp.dot(xcat, cc_ref[j], preferred_element_type=F32) + d_ref[:, sl] * u[:, sl]
        y_ref[:, sl] = _gelu(y).astype(y_ref.dtype)

    @pl.when(i == pl.num_programs(0) - 1)
    def _():
        sre_ref[...] = bre[:, last_row:last_row + 1, :]
        sim_ref[...] = bim[:, last_row:last_row + 1, :]


def _s5_prompt(u, a_re, a_im, wb, cc, d, n_valid, tt):
    n, dm = u.shape
    nt, _, two_w = wb.shape
    spt = two_w // 2 // LANES
    n_slabs = nt * spt
    seg = tt // SUBLANES
    nblk = n // tt
    assert (n_valid - 1) // tt == nblk - 1
    kern = functools.partial(_s5_prompt_kernel, tt=tt, seg=seg, last_row=(n_valid - 1) % tt, spt=spt)
    slab = jax.ShapeDtypeStruct((n_slabs, 1, LANES), F32)
    const3 = lambda i: (0, 0, 0)
    y, sre, sim = pl.pallas_call(
        kern, grid=(nblk,),
        in_specs=[pl.BlockSpec((tt, dm), lambda i: (i, 0)),
                  pl.BlockSpec(wb.shape, const3),
                  pl.BlockSpec(cc.shape, const3),
                  pl.BlockSpec((n_slabs, 1, LANES), const3),
                  pl.BlockSpec((n_slabs, 1, LANES), const3),
                  pl.BlockSpec((1, dm), lambda i: (0, 0))],
        out_specs=[pl.BlockSpec((tt, dm), lambda i: (i, 0)),
                   pl.BlockSpec((n_slabs, 1, LANES), const3),
                   pl.BlockSpec((n_slabs, 1, LANES), const3)],
        out_shape=[jax.ShapeDtypeStruct((n, dm), BF16), slab, slab],
        scratch_shapes=[pltpu.VMEM((n_slabs, tt, LANES), F32), pltpu.VMEM((n_slabs, tt, LANES), F32),
                        pltpu.VMEM((n_slabs, seg, LANES), F32), pltpu.VMEM((n_slabs, seg, LANES), F32),
                        pltpu.VMEM((n_slabs, 1, LANES), F32), pltpu.VMEM((n_slabs, 1, LANES), F32)],
        compiler_params=_cparams(1),
        name="s5_prompt",
    )(u, wb, cc, a_re.reshape(n_slabs, 1, LANES), a_im.reshape(n_slabs, 1, LANES), d.reshape(1, dm))
    return y, sre.reshape(-1), sim.reshape(-1)


def _s5_sample_kernel(u_ref, wb_ref, cc_ref, are_ref, aim_ref, d_ref, s0r_ref, s0i_ref,
                      y_ref, sre_ref, sim_ref, bre, bim):
    n_steps = u_ref.shape[0]
    n_tiles = u_ref.shape[2] // LANES
    half = wb_ref.shape[2] // 2
    a_r = are_ref[...]
    a_i = aim_ref[...]
    x_r = s0r_ref[...]
    x_i = s0i_ref[...]
    for t in range(n_steps):
        u = u_ref[t]
        for j in range(n_tiles):
            r = jnp.dot(u[:, j * LANES:(j + 1) * LANES].astype(BF16), wb_ref[j], preferred_element_type=F32)
            bre[:, j * half:(j + 1) * half] = r[:, :half]
            bim[:, j * half:(j + 1) * half] = r[:, half:]
        x_r, x_i = a_r * x_r - a_i * x_i + bre[...], a_r * x_i + a_i * x_r + bim[...]
        for j in range(n_tiles):
            xcat = jnp.concatenate([x_r[:, j * half:(j + 1) * half], x_i[:, j * half:(j + 1) * half]],
                                   axis=1).astype(BF16)
            sl = slice(j * LANES, (j + 1) * LANES)
            y = jnp.dot(xcat, cc_ref[j], preferred_element_type=F32) + d_ref[:, sl] * u[:, sl]
            y_ref[t, :, sl] = _gelu(y).astype(y_ref.dtype)
    sre_ref[...] = x_r
    sim_ref[...] = x_i


def _s5_sample(u, a_re, a_im, wb, cc, d, s0_re, s0_im):
    n_steps, nb, dm = u.shape
    width = s0_re.shape[1]
    tb = _pick_tile(nb, (32, 16, 8))
    const3 = lambda i: (0, 0, 0)
    st = jax.ShapeDtypeStruct((nb, width), F32)
    return pl.pallas_call(
        _s5_sample_kernel, grid=(nb // tb,),
        in_specs=[pl.BlockSpec((n_steps, tb, dm), lambda i: (0, i, 0)),
                  pl.BlockSpec(wb.shape, const3),
                  pl.BlockSpec(cc.shape, const3),
                  pl.BlockSpec((1, width), lambda i: (0, 0)),
                  pl.BlockSpec((1, width), lambda i: (0, 0)),
                  pl.BlockSpec((1, dm), lambda i: (0, 0)),
                  pl.BlockSpec((tb, width), lambda i: (i, 0)),
                  pl.BlockSpec((tb, width), lambda i: (i, 0))],
        out_specs=[pl.BlockSpec((n_steps, tb, dm), lambda i: (0, i, 0)),
                   pl.BlockSpec((tb, width), lambda i: (i, 0)),
                   pl.BlockSpec((tb, width), lambda i: (i, 0))],
        out_shape=[jax.ShapeDtypeStruct((n_steps, nb, dm), BF16), st, st],
        scratch_shapes=[pltpu.VMEM((tb, width), F32), pltpu.VMEM((tb, width), F32)],
        compiler_params=_cparams(1),
        name="s5_sample",
    )(u, wb, cc, a_re.reshape(1, width), a_im.reshape(1, width), d.reshape(1, dm), s0_re, s0_im)


def _topk_rows(s, k):
    n = s.shape[0]
    row = lax.broadcasted_iota(jnp.int32, s.shape, 0)
    vals, idxs = [], []
    for t in range(k):
        m = jnp.max(s, axis=0, keepdims=True)
        idx = jnp.min(jnp.where(s == m, row, n), axis=0, keepdims=True)
        vals.append(m)
        idxs.append(idx)
        if t + 1 < k:
            s = jnp.where(row == idx, -jnp.inf, s)
    return jnp.concatenate(vals, axis=0), jnp.concatenate(idxs, axis=0)


def _candidate_pairs(k):
    return [(i, j) for i in range(k) for j in range(k) if (i + 1) * (j + 1) <= k]


def _peer_router_kernel(x_ref, g_ref, wq_ref, keys_ref, h_ref, gt_ref, i1_ref, i2_ref, q_s,
                        *, n_heads, topk):
    x = x_ref[...]
    hb = (x * _rms_scale(x) * g_ref[...]).astype(BF16)
    h_ref[...] = hb
    q_s[...] = jnp.dot(hb, wq_ref[...], preferred_element_type=F32).astype(BF16)
    tm = x.shape[0]
    n_keys, half = keys_ref.shape[1], keys_ref.shape[2]
    n_chunks = tm // LANES
    pairs = _candidate_pairs(topk)
    counts = [sum(1 for p in pairs if p[0] == i) for i in range(topk)]
    n_cand = _round_up(len(pairs), SUBLANES)

    def body(it, carry):
        hd = it % n_heads
        r0 = pl.multiple_of((it // n_heads) * LANES, LANES)
        q = q_s[pl.ds(r0, LANES), pl.ds(pl.multiple_of(hd * 2 * half, 2 * half), 2 * half)]
        s0 = lax.dot_general(keys_ref[0], q[:, :half], NT_DIMS, preferred_element_type=F32)
        s1 = lax.dot_general(keys_ref[1], q[:, half:], NT_DIMS, preferred_element_type=F32)
        sv0, si0 = _topk_rows(s0, topk)
        sv1, si1 = _topk_rows(s1, topk)
        si0 = si0.astype(F32) * float(n_keys)
        si1 = si1.astype(F32)
        cs, ce = [], []
        for i in range(topk):
            cnt = counts[i]
            cs.append(sv0[i:i + 1] + sv1[0:cnt])
            ce.append(si0[i:i + 1] + si1[0:cnt])
        pad = n_cand - len(pairs)
        if pad:
            cs.append(jnp.full((pad, LANES), -jnp.inf, F32))
            ce.append(jnp.zeros((pad, LANES), F32))
        cand = jnp.concatenate(cs, axis=0)
        cand_e = jnp.concatenate(ce, axis=0)
        row = lax.broadcasted_iota(jnp.int32, cand.shape, 0)
        ts, te = [], []
        for t in range(topk):
            m = jnp.max(cand, axis=0, keepdims=True)
            idx = jnp.min(jnp.where(cand == m, row, n_cand), axis=0, keepdims=True)
            sel = row == idx
            ts.append(m)
            te.append(jnp.sum(jnp.where(sel, cand_e, 0.0), axis=0, keepdims=True))
            if t + 1 < topk:
                cand = jnp.where(sel, -jnp.inf, cand)
        ts = jnp.concatenate(ts, axis=0)
        te = jnp.concatenate(te, axis=0)
        e = jnp.exp(ts - ts[0:1])
        gate = e / jnp.sum(e, axis=0, keepdims=True)
        t1 = jnp.floor(te * (1.0 / n_keys))
        rows = pl.ds(pl.multiple_of(hd * topk, topk), topk)
        cols = pl.ds(r0, LANES)
        gt_ref[rows, cols] = gate
        i1_ref[rows, cols] = t1
        i2_ref[rows, cols] = te - t1 * float(n_keys)
        return carry

    lax.fori_loop(0, n_heads * n_chunks, body, 0, unroll=8)


def _peer_router(x, g, w_q, sub_keys):
    n, d = x.shape
    n_keys, half = sub_keys.shape[1], sub_keys.shape[2]
    assert n_keys == LANES and half == LANES
    n_heads = w_q.shape[1] // (2 * half)
    hk = n_heads * PEER_TOPK
    tm = _pick_tile(n, (640, 512, 256, 128))
    kern = functools.partial(_peer_router_kernel, n_heads=n_heads, topk=PEER_TOPK)
    tr = jax.ShapeDtypeStruct((hk, n), F32)
    return pl.pallas_call(
        kern, grid=(n // tm,),
        in_specs=[pl.BlockSpec((tm, d), lambda i: (i, 0)),
                  pl.BlockSpec((1, d), lambda i: (0, 0)),
                  pl.BlockSpec(w_q.shape, lambda i: (0, 0)),
                  pl.BlockSpec(sub_keys.shape, lambda i: (0, 0, 0))],
        out_specs=[pl.BlockSpec((tm, d), lambda i: (i, 0)),
                   pl.BlockSpec((hk, tm), lambda i: (0, i)),
                   pl.BlockSpec((hk, tm), lambda i: (0, i)),
                   pl.BlockSpec((hk, tm), lambda i: (0, i))],
        out_shape=[jax.ShapeDtypeStruct((n, d), BF16), tr, tr, tr],
        scratch_shapes=[pltpu.VMEM((tm, w_q.shape[1]), BF16)],
        compiler_params=_cparams(1),
        name="peer_router",
    )(x, g.reshape(1, d), w_q, sub_keys)


def _peer_expand_kernel(gt_ref, i1_ref, i2_ref, *rest, pitch, cast_tables):
    if cast_tables:
        u_ref, v_ref, w_ref, ub_ref, vb_ref, g_s, a_s, b_s, scr = rest
        ub_ref[...] = u_ref[...].astype(BF16)
        vb_ref[...] = v_ref[...].astype(BF16)
    else:
        w_ref, g_s, a_s, b_s, scr = rest
    g_s[...] = gt_ref[...].T
    a_s[...] = i1_ref[...].T
    b_s[...] = i2_ref[...].T
    tn, hk = g_s.shape
    n_keys = w_ref.shape[1] // LANES
    sub = lax.broadcasted_iota(jnp.int32, (n_keys, hk), 0).astype(F32)

    def body(n, carry):
        at = jnp.where(a_s[pl.ds(n, 1), :] == sub, g_s[pl.ds(n, 1), :], 0.0).astype(BF16)
        bt = jnp.where(b_s[pl.ds(n, 1), :] == sub, 1.0, 0.0).astype(BF16)
        scr[pl.ds(pl.multiple_of(n * pitch, SUBLANES), n_keys), :] = lax.dot_general(
            at, bt, NT_DIMS, preferred_element_type=F32)
        return carry

    lax.fori_loop(0, tn, body, 0, unroll=16)
    for a in range(n_keys):
        w_ref[:, a * LANES:(a + 1) * LANES] = scr[pl.ds(a, tn, stride=pitch), :].astype(w_ref.dtype)


def _peer_expand(gt, i1, i2, n_keys, tables=None):
    hk, n = gt.shape
    assert n_keys == LANES and hk == LANES
    tn = LANES
    steps = n // tn
    pitch = n_keys + SUBLANES
    kern = functools.partial(_peer_expand_kernel, pitch=pitch, cast_tables=tables is not None)
    spec = pl.BlockSpec((hk, tn), lambda i: (0, i))
    in_specs = [spec, spec, spec]
    out_specs = [pl.BlockSpec((tn, n_keys * n_keys), lambda i: (i, 0))]
    out_shape = [jax.ShapeDtypeStruct((n, n_keys * n_keys), BF16)]
    args = [gt, i1, i2]
    if tables is not None:
        layer, u_all, v_all = tables
        _, n_exp, d = u_all.shape
        nb = 1 << (steps.bit_length() - 1)
        assert n_exp % nb == 0
        tr = n_exp // nb
        in_specs += [pl.BlockSpec((None, tr, d), lambda i: (layer, jnp.minimum(i, nb - 1), 0))] * 2
        out_specs += [pl.BlockSpec((tr, d), lambda i: (jnp.minimum(i, nb - 1), 0))] * 2
        out_shape += [jax.ShapeDtypeStruct((n_exp, d), BF16)] * 2
        args += [u_all, v_all]
    return pl.pallas_call(
        kern, grid=(steps,),
        in_specs=in_specs, out_specs=out_specs, out_shape=out_shape,
        scratch_shapes=[pltpu.VMEM((tn, hk), F32), pltpu.VMEM((tn, hk), F32), pltpu.VMEM((tn, hk), F32),
                        pltpu.VMEM((tn * pitch, LANES), F32)],
        compiler_params=_cparams(1),
        name="peer_expand",
    )(*args)


def _peer_dense_kernel(h_ref, u_ref, v_ref, w_ref, r_ref, o_ref):
    @pl.when(pl.program_id(1) == 0)
    def _():
        o_ref[...] = r_ref[...]

    act = lax.dot_general(h_ref[...], u_ref[...], NT_DIMS, preferred_element_type=F32)
    p = (w_ref[...].astype(F32) * _gelu(act)).astype(BF16)
    o_ref[...] += jnp.dot(p, v_ref[...], preferred_element_type=F32)


def _peer_dense(h, u_tab, v_tab, w, res):
    n, d = h.shape
    n_exp = u_tab.shape[0]
    tm = _pick_tile(n, (640, 512, 256, 128))
    te = 1024
    once = pl.Buffered(1)
    return pl.pallas_call(
        _peer_dense_kernel, grid=(n // tm, n_exp // te),
        in_specs=[pl.BlockSpec((tm, d), lambda i, e: (i, 0), pipeline_mode=once),
                  pl.BlockSpec((te, d), lambda i, e: (e, 0)),
                  pl.BlockSpec((te, d), lambda i, e: (e, 0)),
                  pl.BlockSpec((tm, te), lambda i, e: (i, e)),
                  pl.BlockSpec((tm, d), lambda i, e: (i, 0), pipeline_mode=once)],
        out_specs=pl.BlockSpec((tm, d), lambda i, e: (i, 0)),
        out_shape=jax.ShapeDtypeStruct((n, d), F32),
        compiler_params=_cparams(2),
        name="peer_dense",
    )(h, u_tab, v_tab, w, res)


def _peer_ffn(x, g, w_q, sub_keys, tables, tables_are_f32):
    h, gt, i1, i2 = _peer_router(x, g, w_q, sub_keys)
    if tables_are_f32:
        w, u_tab, v_tab = _peer_expand(gt, i1, i2, sub_keys.shape[1], tables)
    else:
        (w,) = _peer_expand(gt, i1, i2, sub_keys.shape[1])
        u_tab, v_tab = tables
    return _peer_dense(h, u_tab, v_tab, w, x), (u_tab, v_tab)


def kernel(x_prompt, x_sample, cache_k, cache_v, state_ssm_re, state_ssm_im, page_table, meta_tokens, norm_mix, norm_ffn, norm_final, attn_w_qkv, attn_lambda, attn_subln, attn_w_o, ssm_w_in, ssm_lambda_re, ssm_lambda_im, ssm_log_dt, ssm_b_re, ssm_b_im, ssm_c_re, ssm_c_im, ssm_d, ssm_w_glu, peer_w_q, peer_sub_keys, peer_u, peer_v):
    batch, seq, d_model = x_prompt.shape
    assert batch == 1
    dec_batch, dec_seq, _ = x_sample.shape
    n_meta = meta_tokens.shape[0]
    depth = norm_mix.shape[0]
    n_heads, hw = cache_k.shape[3], cache_k.shape[4]
    hd = hw // 2
    n_groups, state_dim = ssm_lambda_re.shape[1], ssm_lambda_re.shape[2]
    t_tot = n_meta + seq
    tq = 640
    tt = 160
    n_p = _round_up(t_tot, math.lcm(tq, tt))
    n_s = dec_batch * dec_seq
    slopes = jnp.asarray(2.0 ** (-8.0 * np.arange(1, n_heads + 1) / n_heads) * LOG2E, dtype=F32)

    xp = jnp.concatenate([meta_tokens.astype(F32), x_prompt[0],
                          jnp.zeros((n_p - t_tot, d_model), F32)], axis=0)
    xs = x_sample.reshape(n_s, d_model)

    outs = {k: [] for k in ("kp", "vp", "ks", "vs", "srp", "sip", "srs", "sis")}
    for i in range(depth):
        j = i // 2
        if i % 2 == 0:
            lam_init = _lambda_init(i)
            w_qkv = attn_w_qkv[j].astype(BF16)
            w_o = attn_w_o[j].astype(BF16)
            qk_w = n_heads * hw
            qkv_p, k_p, v_p = _qkv_matmul(xp, norm_mix[i], w_qkv, t_tot)
            qkv_s, k_s, v_s = _qkv_matmul(xs, norm_mix[i], w_qkv, n_s)
            o_p = _attn_prompt(qkv_p, attn_lambda[j], attn_subln[j], slopes, n_heads, hd, lam_init, tq)

            def rows(a):
                return a.reshape(dec_batch, dec_seq * n_heads, hw)

            o_s = _attn_sample(rows(qkv_s[:, :qk_w]), rows(k_s), rows(v_s),
                               cache_k, cache_v, j, page_table, attn_lambda[j], attn_subln[j], slopes,
                               n_heads, hd, lam_init)
            xp = _matmul_residual(o_p, w_o, xp)
            xs = _matmul_residual(o_s.reshape(n_s, qk_w), w_o, xs)
            outs["kp"].append(k_p.reshape(batch, t_tot, n_heads, hw))
            outs["vp"].append(v_p.reshape(batch, t_tot, n_heads, hw))
            outs["ks"].append(k_s.reshape(dec_batch, dec_seq, n_heads, hw))
            outs["vs"].append(v_s.reshape(dec_batch, dec_seq, n_heads, hw))
        else:
            w_in = ssm_w_in[j].astype(BF16)
            w_glu = ssm_w_glu[j].astype(BF16)
            a_re, a_im, wb, cc = _s5_weights(ssm_lambda_re[j], ssm_lambda_im[j], ssm_log_dt[j],
                                             ssm_b_re[j], ssm_b_im[j], ssm_c_re[j], ssm_c_im[j])
            d_vec = ssm_d[j].reshape(-1)
            u_p = _norm_matmul(xp, norm_mix[i], w_in)
            y_p, sr_p, si_p = _s5_prompt(u_p, a_re, a_im, wb, cc, d_vec, t_tot, tt)
            xp = _glu_residual(y_p, w_glu, xp)
            u_s = _norm_matmul(xs, norm_mix[i], w_in)
            u_s = u_s.reshape(dec_batch, dec_seq, d_model).transpose(1, 0, 2)
            y_s, sr_s, si_s = _s5_sample(u_s, a_re, a_im, wb, cc, d_vec,
                                         state_ssm_re[j].reshape(dec_batch, -1),
                                         state_ssm_im[j].reshape(dec_batch, -1))
            xs = _glu_residual(y_s.transpose(1, 0, 2).reshape(n_s, d_model), w_glu, xs)
            outs["srp"].append(sr_p.reshape(batch, n_groups, state_dim))
            outs["sip"].append(si_p.reshape(batch, n_groups, state_dim))
            outs["srs"].append(sr_s.reshape(dec_batch, n_groups, state_dim))
            outs["sis"].append(si_s.reshape(dec_batch, n_groups, state_dim))
        w_q = peer_w_q[i].astype(BF16)
        keys = peer_sub_keys[i].astype(BF16)
        xp, tables = _peer_ffn(xp, norm_ffn[i], w_q, keys, (i, peer_u, peer_v), True)
        xs, _ = _peer_ffn(xs, norm_ffn[i], w_q, keys, tables, False)

    y_prompt = _final_norm(xp, norm_final)[n_meta:t_tot].reshape(batch, seq, d_model)
    y_sample = _final_norm(xs, norm_final).reshape(dec_batch, dec_seq, d_model)
    return (y_prompt, y_sample,
            jnp.stack(outs["kp"]), jnp.stack(outs["vp"]), jnp.stack(outs["srp"]), jnp.stack(outs["sip"]),
            jnp.stack(outs["ks"]), jnp.stack(outs["vs"]), jnp.stack(outs["srs"]), jnp.stack(outs["sis"]))
```

```python
import functools
import math

import numpy as np
import jax
import jax.numpy as jnp
from jax import lax
from jax.experimental import pallas as pl
from jax.experimental.pallas import tpu as pltpu

F32 = jnp.float32
BF16 = jnp.bfloat16
EPS = 1e-6
PEER_TOPK = 16
LANES = 128
SUBLANES = 8
NEG_BIG = -1e30
LOG2E = math.log2(math.e)
VMEM_LIMIT = 56 * 1024 * 1024
NT_DIMS = (((1,), (1,)), ((), ()))


def _cparams(n_axes):
    return pltpu.CompilerParams(dimension_semantics=("arbitrary",) * n_axes,
                                vmem_limit_bytes=VMEM_LIMIT)


def _round_up(x, m):
    return (x + m - 1) // m * m


def _pick_tile(n, candidates):
    for c in candidates:
        if n % c == 0:
            return c
    return n


def _gelu(x):
    return 0.5 * x * (1.0 + lax.erf(x * (1.0 / math.sqrt(2.0))))


def _rms_scale(x):
    return lax.rsqrt(jnp.mean(x * x, axis=-1, keepdims=True) + EPS)


def _lambda_init(layer):
    return 0.8 - 0.6 * math.exp(-0.3 * layer)


def _norm_mm_kernel(x_ref, g_ref, w_ref, o_ref, h_ref):
    @pl.when(pl.program_id(1) == 0)
    def _():
        x = x_ref[...]
        h_ref[...] = (x * _rms_scale(x) * g_ref[...]).astype(BF16)

    o_ref[...] = jnp.dot(h_ref[...], w_ref[...], preferred_element_type=F32)


def _norm_matmul(x, g, w):
    n, d = x.shape
    m = w.shape[1]
    tm = _pick_tile(n, (640, 512, 256, 128))
    tn = _pick_tile(m, (2048, 1024, 512, 256, 128))
    return pl.pallas_call(
        _norm_mm_kernel,
        grid=(n // tm, m // tn),
        in_specs=[pl.BlockSpec((tm, d), lambda i, j: (i, 0)),
                  pl.BlockSpec((1, d), lambda i, j: (0, 0)),
                  pl.BlockSpec((d, tn), lambda i, j: (0, j))],
        out_specs=pl.BlockSpec((tm, tn), lambda i, j: (i, j)),
        out_shape=jax.ShapeDtypeStruct((n, m), F32),
        scratch_shapes=[pltpu.VMEM((tm, d), BF16)],
        compiler_params=_cparams(2),
        name="norm_matmul",
    )(x, g.reshape(1, d), w)


def _qkv_kernel(x_ref, g_ref, w_ref, o_ref, ko_ref, vo_ref, h_ref, *, nq):
    j = pl.program_id(1)

    @pl.when(j == 0)
    def _():
        x = x_ref[...]
        h_ref[...] = (x * _rms_scale(x) * g_ref[...]).astype(BF16)

    acc = jnp.dot(h_ref[...], w_ref[...], preferred_element_type=F32)
    o_ref[...] = acc

    @pl.when(jnp.logical_and(j >= nq, j < 2 * nq))
    def _():
        ko_ref[...] = acc

    @pl.when(j >= 2 * nq)
    def _():
        vo_ref[...] = acc


def _qkv_matmul(x, g, w, n_valid):
    n, d = x.shape
    width = w.shape[1] // 3
    tm = _pick_tile(n, (640, 512, 256, 128))
    tn = _pick_tile(width, (1024, 512, 256, 128))
    nq = width // tn
    part = jax.ShapeDtypeStruct((n_valid, width), F32)
    return pl.pallas_call(
        functools.partial(_qkv_kernel, nq=nq),
        grid=(n // tm, 3 * nq),
        in_specs=[pl.BlockSpec((tm, d), lambda i, j: (i, 0)),
                  pl.BlockSpec((1, d), lambda i, j: (0, 0)),
                  pl.BlockSpec((d, tn), lambda i, j: (0, j))],
        out_specs=[pl.BlockSpec((tm, tn), lambda i, j: (i, j)),
                   pl.BlockSpec((tm, tn), lambda i, j: (i, jnp.clip(j - nq, 0, nq - 1))),
                   pl.BlockSpec((tm, tn), lambda i, j: (i, jnp.clip(j - 2 * nq, 0, nq - 1)))],
        out_shape=[jax.ShapeDtypeStruct((n, 3 * width), F32), part, part],
        scratch_shapes=[pltpu.VMEM((tm, d), BF16)],
        compiler_params=_cparams(2),
        name="qkv_matmul",
    )(x, g.reshape(1, d), w)


def _mm_res_kernel(a_ref, w_ref, r_ref, o_ref):
    o_ref[...] = r_ref[...] + jnp.dot(a_ref[...], w_ref[...], preferred_element_type=F32)


def _matmul_residual(a, w, res):
    n, k = a.shape
    m = w.shape[1]
    tm = _pick_tile(n, (640, 512, 256, 128))
    tn = _pick_tile(m, (2048, 1024, 512, 256, 128))
    return pl.pallas_call(
        _mm_res_kernel,
        grid=(n // tm, m // tn),
        in_specs=[pl.BlockSpec((tm, k), lambda i, j: (i, 0)),
                  pl.BlockSpec((k, tn), lambda i, j: (0, j)),
                  pl.BlockSpec((tm, tn), lambda i, j: (i, j))],
        out_specs=pl.BlockSpec((tm, tn), lambda i, j: (i, j)),
        out_shape=jax.ShapeDtypeStruct((n, m), F32),
        compiler_params=_cparams(2),
        name="matmul_residual",
    )(a, w, res)


def _glu_res_kernel(a_ref, wv_ref, wg_ref, r_ref, o_ref):
    a = a_ref[...]
    zv = jnp.dot(a, wv_ref[...], preferred_element_type=F32)
    zg = jnp.dot(a, wg_ref[...], preferred_element_type=F32)
    o_ref[...] = r_ref[...] + zv * jax.nn.sigmoid(zg)


def _glu_residual(a, w, res):
    n, k = a.shape
    m = w.shape[1] // 2
    tm = _pick_tile(n, (640, 512, 256, 128))
    tn = _pick_tile(m, (1024, 512, 256, 128))
    nj = m // tn
    return pl.pallas_call(
        _glu_res_kernel,
        grid=(n // tm, nj),
        in_specs=[pl.BlockSpec((tm, k), lambda i, j: (i, 0)),
                  pl.BlockSpec((k, tn), lambda i, j: (0, j)),
                  pl.BlockSpec((k, tn), lambda i, j: (0, j + nj)),
                  pl.BlockSpec((tm, tn), lambda i, j: (i, j))],
        out_specs=pl.BlockSpec((tm, tn), lambda i, j: (i, j)),
        out_shape=jax.ShapeDtypeStruct((n, m), F32),
        compiler_params=_cparams(2),
        name="glu_residual",
    )(a, w, w, res)


def _final_norm_kernel(x_ref, g_ref, o_ref):
    x = x_ref[...]
    o_ref[...] = x * _rms_scale(x) * g_ref[...]


def _final_norm(x, g):
    n, d = x.shape
    tm = _pick_tile(n, (640, 512, 256, 128))
    return pl.pallas_call(
        _final_norm_kernel,
        grid=(n // tm,),
        in_specs=[pl.BlockSpec((tm, d), lambda i: (i, 0)),
                  pl.BlockSpec((1, d), lambda i: (0, 0))],
        out_specs=pl.BlockSpec((tm, d), lambda i: (i, 0)),
        out_shape=jax.ShapeDtypeStruct((n, d), F32),
        compiler_params=_cparams(1),
        name="final_norm",
    )(x, g.reshape(1, d))


def _diff_lambda(lam_ref, lam_init):
    lv = lam_ref[...]
    s1 = jnp.sum(lv[0:1] * lv[1:2], axis=1, keepdims=True)
    s2 = jnp.sum(lv[2:3] * lv[3:4], axis=1, keepdims=True)
    return jnp.exp(s1) - jnp.exp(s2) + lam_init


def _softmax_update(sc, v, m_ref, l_ref, a_ref):
    m_old = m_ref[...]
    m_new = jnp.maximum(m_old, jnp.max(sc, axis=1, keepdims=True))
    alpha = jnp.exp2(m_old - m_new)
    p = jnp.exp2(sc - m_new)
    l_ref[...] = alpha * l_ref[...] + jnp.sum(p, axis=1, keepdims=True)
    a_ref[...] = alpha * a_ref[...] + jnp.dot(p.astype(BF16), v, preferred_element_type=F32)
    m_ref[...] = m_new


def _diff_finalize(lam_ref, g_ref, o_ref, l1, a1, l2, a2, lam_init):
    lam = _diff_lambda(lam_ref, lam_init)
    o = a1[...] / l1[...] - lam * (a2[...] / l2[...])
    o_ref[...] = ((o * _rms_scale(o) * g_ref[...]) * (1.0 - lam_init)).astype(o_ref.dtype)


def _init_softmax_state(m1, l1, a1, m2, l2, a2):
    for m, l, a in ((m1, l1, a1), (m2, l2, a2)):
        m[...] = jnp.full(m.shape, NEG_BIG, F32)
        l[...] = jnp.zeros(l.shape, F32)
        a[...] = jnp.zeros(a.shape, F32)


def _flash_prompt_kernel(qi_ref, kj_ref, slope_ref, q_ref, k_ref, v_ref, lam_ref, g_ref, o_ref,
                         qs, m1, l1, a1, m2, l2, a2, *, tq, hd, lam_init):
    h = pl.program_id(0)
    s = pl.program_id(1)
    qi = qi_ref[s]
    kj = kj_ref[s]

    @pl.when(kj == 0)
    def _():
        qs[...] = (q_ref[...] * (hd ** -0.5 * LOG2E)).astype(BF16)
        _init_softmax_state(m1, l1, a1, m2, l2, a2)

    def update(masked):
        k = k_ref[...].astype(BF16)
        v = v_ref[...].astype(BF16)
        col = lax.broadcasted_iota(jnp.int32, (1, tq), 1)
        bias = slope_ref[h] * ((kj - qi) * tq + col).astype(F32)
        if masked:
            keep = (lax.broadcasted_iota(jnp.int32, (tq, tq), 1)
                    <= lax.broadcasted_iota(jnp.int32, (tq, tq), 0))
        for lo, m_ref, l_ref, a_ref in ((0, m1, l1, a1), (hd, m2, l2, a2)):
            sc = lax.dot_general(qs[:, lo:lo + hd], k[:, lo:lo + hd], NT_DIMS,
                                 preferred_element_type=F32) + bias
            if masked:
                sc = jnp.where(keep, sc, NEG_BIG)
            _softmax_update(sc, v, m_ref, l_ref, a_ref)

    @pl.when(kj < qi)
    def _():
        update(False)

    @pl.when(kj == qi)
    def _():
        update(True)
        _diff_finalize(lam_ref, g_ref, o_ref, l1, a1, l2, a2, lam_init)


def _attn_prompt(qkv, lam_vecs, subln, slopes, n_heads, hd, lam_init, tq):
    n = qkv.shape[0]
    nb = n // tq
    hw = 2 * hd
    pairs = [(i, j) for i in range(nb) for j in range(i + 1)]
    qi = jnp.asarray([p[0] for p in pairs], jnp.int32)
    kj = jnp.asarray([p[1] for p in pairs], jnp.int32)
    kern = functools.partial(_flash_prompt_kernel, tq=tq, hd=hd, lam_init=lam_init)
    grid_spec = pltpu.PrefetchScalarGridSpec(
        num_scalar_prefetch=3,
        grid=(n_heads, len(pairs)),
        in_specs=[pl.BlockSpec((tq, hw), lambda h, s, qi, kj, sl: (qi[s], h)),
                  pl.BlockSpec((tq, hw), lambda h, s, qi, kj, sl: (kj[s], n_heads + h)),
                  pl.BlockSpec((tq, hw), lambda h, s, qi, kj, sl: (kj[s], 2 * n_heads + h)),
                  pl.BlockSpec((4, hd), lambda h, s, qi, kj, sl: (0, 0)),
                  pl.BlockSpec((1, hw), lambda h, s, qi, kj, sl: (0, 0))],
        out_specs=pl.BlockSpec((tq, hw), lambda h, s, qi, kj, sl: (qi[s], h)),
        scratch_shapes=[pltpu.VMEM((tq, hw), BF16),
                        pltpu.VMEM((tq, 1), F32), pltpu.VMEM((tq, 1), F32), pltpu.VMEM((tq, hw), F32),
                        pltpu.VMEM((tq, 1), F32), pltpu.VMEM((tq, 1), F32), pltpu.VMEM((tq, hw), F32)])
    return pl.pallas_call(
        kern, grid_spec=grid_spec,
        out_shape=jax.ShapeDtypeStruct((n, n_heads * hw), BF16),
        compiler_params=_cparams(2),
        name="attn_prompt",
    )(qi, kj, slopes, qkv, qkv, qkv, lam_vecs, subln.reshape(1, hw))


def _attn_sample_kernel(pt_ref, q_ref, kn_ref, vn_ref, slope_ref, lam_ref, g_ref, *rest,
                        pps, hd, n_heads, page, past_len, lam_init):
    page_refs = rest[:2 * pps]
    o_ref = rest[2 * pps]
    qs, kf_s, vf_s, m_s, l_s, a_s = rest[2 * pps + 1:]
    p = pl.program_id(1)
    rows = q_ref.shape[0]
    prow = page * n_heads

    @pl.when(p == 0)
    def _():
        q = (q_ref[...] * (hd ** -0.5 * LOG2E)).astype(BF16)
        zero = jnp.zeros((rows, hd), BF16)
        qs[0:rows, :] = jnp.concatenate([q[:, :hd], zero], axis=1)
        qs[rows:, :] = jnp.concatenate([zero, q[:, hd:]], axis=1)
        m_s[...] = jnp.full(m_s.shape, NEG_BIG, F32)
        l_s[...] = jnp.zeros(l_s.shape, F32)
        a_s[...] = jnp.zeros(a_s.shape, F32)

    row = lax.broadcasted_iota(jnp.int32, (2 * rows, 1), 0) % rows
    t_row = row // n_heads
    h_row = row % n_heads
    slope = jnp.concatenate([slope_ref[...], slope_ref[...]], axis=0)

    def attend(kf, vf, first_pos, causal):
        cols = kf.shape[0]
        col = lax.broadcasted_iota(jnp.int32, (1, cols), 1)
        tok = col // n_heads
        valid = h_row == col % n_heads
        if causal:
            valid = jnp.logical_and(valid, tok <= t_row)
        bias = slope * (first_pos - past_len + tok).astype(F32)
        sc = lax.dot_general(qs[...], kf, NT_DIMS, preferred_element_type=F32) + bias
        _softmax_update(jnp.where(valid, sc, NEG_BIG), vf, m_s, l_s, a_s)

    for i in range(pps):
        kf_s[i * prow:(i + 1) * prow, :] = page_refs[i][...].reshape(prow, 2 * hd).astype(BF16)
        vf_s[i * prow:(i + 1) * prow, :] = page_refs[pps + i][...].reshape(prow, 2 * hd).astype(BF16)
    attend(kf_s[...], vf_s[...], p * pps * page, False)

    @pl.when(p == pl.num_programs(1) - 1)
    def _():
        attend(kn_ref[...].astype(BF16), vn_ref[...].astype(BF16), past_len, True)
        top, bot = pl.ds(0, rows), pl.ds(rows, rows)
        _diff_finalize(lam_ref, g_ref, o_ref, l_s.at[top], a_s.at[top], l_s.at[bot], a_s.at[bot], lam_init)


def _attn_sample(q, k_new, v_new, cache_k, cache_v, layer, page_table, lam_vecs, subln, slopes,
                 n_heads, hd, lam_init):
    nb, rows, hw = q.shape
    page = cache_k.shape[2]
    n_pages = page_table.shape[1]
    pps = _pick_tile(n_pages, (8, 4, 2, 1))
    kern = functools.partial(_attn_sample_kernel, pps=pps, hd=hd, n_heads=n_heads, page=page,
                             past_len=n_pages * page, lam_init=lam_init)
    slope_rows = jnp.tile(slopes, rows // n_heads).reshape(rows, 1)

    def page_spec(i):
        return pl.BlockSpec((None, None, page, n_heads, hw),
                            lambda b, p, pt: (layer, pt[b * n_pages + p * pps + i], 0, 0, 0))

    row_spec = pl.BlockSpec((None, rows, hw), lambda b, p, pt: (b, 0, 0))
    grid_spec = pltpu.PrefetchScalarGridSpec(
        num_scalar_prefetch=1,
        grid=(nb, n_pages // pps),
        in_specs=[row_spec, row_spec, row_spec,
                  pl.BlockSpec((rows, 1), lambda b, p, pt: (0, 0)),
                  pl.BlockSpec((4, hd), lambda b, p, pt: (0, 0)),
                  pl.BlockSpec((1, hw), lambda b, p, pt: (0, 0))]
                 + [page_spec(i) for i in range(pps)] * 2,
        out_specs=row_spec,
        scratch_shapes=[pltpu.VMEM((2 * rows, hw), BF16),
                        pltpu.VMEM((pps * page * n_heads, hw), BF16),
                        pltpu.VMEM((pps * page * n_heads, hw), BF16),
                        pltpu.VMEM((2 * rows, 1), F32), pltpu.VMEM((2 * rows, 1), F32),
                        pltpu.VMEM((2 * rows, hw), F32)])
    return pl.pallas_call(
        kern, grid_spec=grid_spec,
        out_shape=jax.ShapeDtypeStruct((nb, rows, hw), BF16),
        compiler_params=_cparams(2),
        name="attn_sample",
    )(page_table.reshape(-1), q, k_new, v_new, slope_rows, lam_vecs, subln.reshape(1, hw),
      *([cache_k] * pps), *([cache_v] * pps))


def _s5_discretize(lam_re, lam_im, log_dt, b_re, b_im):
    dt = jnp.exp(log_dt)[:, None]
    mag = jnp.exp(lam_re * dt)
    a_re = mag * jnp.cos(lam_im * dt)
    a_im = mag * jnp.sin(lam_im * dt)
    den = lam_re * lam_re + lam_im * lam_im
    nr = a_re - 1.0
    coef_re = (nr * lam_re + a_im * lam_im) / den
    coef_im = (a_im * lam_re - nr * lam_im) / den
    bb_re = coef_re[..., None] * b_re - coef_im[..., None] * b_im
    bb_im = coef_re[..., None] * b_im + coef_im[..., None] * b_re
    return a_re, a_im, bb_re, bb_im


def _s5_weights(lam_re, lam_im, log_dt, b_re, b_im, c_re, c_im):
    g, p = lam_re.shape
    c = b_re.shape[2]
    gpt = LANES // c
    nt = g // gpt
    a_re, a_im, bb_re, bb_im = _s5_discretize(lam_re, lam_im, log_dt, b_re, b_im)
    eye = jnp.eye(gpt, dtype=F32)

    def in_proj(bb):
        t = bb.reshape(nt, gpt, p, c)
        return jnp.einsum('jgpc,gh->jgchp', t, eye).reshape(nt, gpt * c, gpt * p)

    def out_proj(cm):
        t = cm.reshape(nt, gpt, c, p)
        return jnp.einsum('jgcp,gh->jgphc', t, eye).reshape(nt, gpt * p, gpt * c)

    wb = jnp.concatenate([in_proj(bb_re), in_proj(bb_im)], axis=2).astype(BF16)
    cc = jnp.concatenate([out_proj(c_re), -out_proj(c_im)], axis=1).astype(BF16)
    return a_re.reshape(-1), a_im.reshape(-1), wb, cc


def _s5_prompt_kernel(u_ref, wb_ref, cc_ref, are_ref, aim_ref, d_ref, y_ref, sre_ref, sim_ref,
                      bre, bim, pre, pim, cre, cim, *, tt, seg, last_row, spt):
    i = pl.program_id(0)
    n_slabs = bre.shape[0]
    n_tiles = u_ref.shape[1] // LANES

    @pl.when(i == 0)
    def _():
        cre[...] = jnp.zeros(cre.shape, F32)
        cim[...] = jnp.zeros(cim.shape, F32)
        a_r = are_ref[...]
        a_i = aim_ref[...]
        p_r, p_i = a_r, a_i
        for k in range(seg):
            pre[:, k:k + 1, :] = p_r
            pim[:, k:k + 1, :] = p_i
            p_r, p_i = p_r * a_r - p_i * a_i, p_r * a_i + p_i * a_r

    u = u_ref[...]
    for j in range(n_tiles):
        r = jnp.dot(u[:, j * LANES:(j + 1) * LANES].astype(BF16), wb_ref[j], preferred_element_type=F32)
        for q in range(spt):
            bre[j * spt + q] = r[:, q * LANES:(q + 1) * LANES]
            bim[j * spt + q] = r[:, (spt + q) * LANES:(spt + q + 1) * LANES]

    def slab_body(s, carry):
        a_r = jnp.broadcast_to(are_ref[s], (SUBLANES, LANES))
        a_i = jnp.broadcast_to(aim_ref[s], (SUBLANES, LANES))
        x_r = jnp.zeros((SUBLANES, LANES), F32)
        x_i = jnp.zeros((SUBLANES, LANES), F32)
        local = []
        for k in range(seg):
            b_r = bre[s, pl.ds(k, SUBLANES, stride=seg), :]
            b_i = bim[s, pl.ds(k, SUBLANES, stride=seg), :]
            x_r, x_i = a_r * x_r - a_i * x_i + b_r, a_r * x_i + a_i * x_r + b_i
            local.append((x_r, x_i))
        as_r = pre[s, seg - 1:seg, :]
        as_i = pim[s, seg - 1:seg, :]
        s_r = cre[s]
        s_i = cim[s]
        st_r, st_i = [], []
        for g in range(SUBLANES):
            st_r.append(s_r)
            st_i.append(s_i)
            e_r = x_r[g:g + 1]
            e_i = x_i[g:g + 1]
            s_r, s_i = as_r * s_r - as_i * s_i + e_r, as_r * s_i + as_i * s_r + e_i
        cre[s] = s_r
        cim[s] = s_i
        st_r = jnp.concatenate(st_r, axis=0)
        st_i = jnp.concatenate(st_i, axis=0)
        for k in range(seg):
            p_r = jnp.broadcast_to(pre[s, k:k + 1, :], (SUBLANES, LANES))
            p_i = jnp.broadcast_to(pim[s, k:k + 1, :], (SUBLANES, LANES))
            l_r, l_i = local[k]
            bre[s, pl.ds(k, SUBLANES, stride=seg), :] = l_r + p_r * st_r - p_i * st_i
            bim[s, pl.ds(k, SUBLANES, stride=seg), :] = l_i + p_r * st_i + p_i * st_r
        return carry

    lax.fori_loop(0, n_slabs, slab_body, 0, unroll=4)

    for j in range(n_tiles):
        xcat = jnp.concatenate([bre[j * spt + q] for q in range(spt)]
                               + [bim[j * spt + q] for q in range(spt)], axis=1).astype(BF16)
        sl = slice(j * LANES, (j + 1) * LANES)
        y = jnp.dot(xcat, cc_ref[j], preferred_element_type=F32) + d_ref[:, sl] * u[:, sl]
        y_ref[:, sl] = _gelu(y).astype(y_ref.dtype)

    @pl.when(i == pl.num_programs(0) - 1)
    def _():
        sre_ref[...] = bre[:, last_row:last_row + 1, :]
        sim_ref[...] = bim[:, last_row:last_row + 1, :]


def _s5_prompt(u, a_re, a_im, wb, cc, d, n_valid, tt):
    n, dm = u.shape
    nt, _, two_w = wb.shape
    spt = two_w // 2 // LANES
    n_slabs = nt * spt
    seg = tt // SUBLANES
    nblk = n // tt
    assert (n_valid - 1) // tt == nblk - 1
    kern = functools.partial(_s5_prompt_kernel, tt=tt, seg=seg, last_row=(n_valid - 1) % tt, spt=spt)
    slab = jax.ShapeDtypeStruct((n_slabs, 1, LANES), F32)
    const3 = lambda i: (0, 0, 0)
    y, sre, sim = pl.pallas_call(
        kern, grid=(nblk,),
        in_specs=[pl.BlockSpec((tt, dm), lambda i: (i, 0)),
                  pl.BlockSpec(wb.shape, const3),
                  pl.BlockSpec(cc.shape, const3),
                  pl.BlockSpec((n_slabs, 1, LANES), const3),
                  pl.BlockSpec((n_slabs, 1, LANES), const3),
                  pl.BlockSpec((1, dm), lambda i: (0, 0))],
        out_specs=[pl.BlockSpec((tt, dm), lambda i: (i, 0)),
                   pl.BlockSpec((n_slabs, 1, LANES), const3),
                   pl.BlockSpec((n_slabs, 1, LANES), const3)],
        out_shape=[jax.ShapeDtypeStruct((n, dm), BF16), slab, slab],
        scratch_shapes=[pltpu.VMEM((n_slabs, tt, LANES), F32), pltpu.VMEM((n_slabs, tt, LANES), F32),
                        pltpu.VMEM((n_slabs, seg, LANES), F32), pltpu.VMEM((n_slabs, seg, LANES), F32),
                        pltpu.VMEM((n_slabs, 1, LANES), F32), pltpu.VMEM((n_slabs, 1, LANES), F32)],
        compiler_params=_cparams(1),
        name="s5_prompt",
    )(u, wb, cc, a_re.reshape(n_slabs, 1, LANES), a_im.reshape(n_slabs, 1, LANES), d.reshape(1, dm))
    return y, sre.reshape(-1), sim.reshape(-1)


def _s5_sample_kernel(u_ref, wb_ref, cc_ref, are_ref, aim_ref, d_ref, s0r_ref, s0i_ref,
                      y_ref, sre_ref, sim_ref, bre, bim):
    n_steps = u_ref.shape[0]
    n_tiles = u_ref.shape[2] // LANES
    half = wb_ref.shape[2] // 2
    a_r = are_ref[...]
    a_i = aim_ref[...]
    x_r = s0r_ref[...]
    x_i = s0i_ref[...]
    for t in range(n_steps):
        u = u_ref[t]
        for j in range(n_tiles):
            r = jnp.dot(u[:, j * LANES:(j + 1) * LANES].astype(BF16), wb_ref[j], preferred_element_type=F32)
            bre[:, j * half:(j + 1) * half] = r[:, :half]
            bim[:, j * half:(j + 1) * half] = r[:, half:]
        x_r, x_i = a_r * x_r - a_i * x_i + bre[...], a_r * x_i + a_i * x_r + bim[...]
        for j in range(n_tiles):
            xcat = jnp.concatenate([x_r[:, j * half:(j + 1) * half], x_i[:, j * half:(j + 1) * half]],
                                   axis=1).astype(BF16)
            sl = slice(j * LANES, (j + 1) * LANES)
            y = jnp.dot(xcat, cc_ref[j], preferred_element_type=F32) + d_ref[:, sl] * u[:, sl]
            y_ref[t, :, sl] = _gelu(y).astype(y_ref.dtype)
    sre_ref[...] = x_r
    sim_ref[...] = x_i


def _s5_sample(u, a_re, a_im, wb, cc, d, s0_re, s0_im):
    n_steps, nb, dm = u.shape
    width = s0_re.shape[1]
    tb = _pick_tile(nb, (32, 16, 8))
    const3 = lambda i: (0, 0, 0)
    st = jax.ShapeDtypeStruct((nb, width), F32)
    return pl.pallas_call(
        _s5_sample_kernel, grid=(nb // tb,),
        in_specs=[pl.BlockSpec((n_steps, tb, dm), lambda i: (0, i, 0)),
                  pl.BlockSpec(wb.shape, const3),
                  pl.BlockSpec(cc.shape, const3),
                  pl.BlockSpec((1, width), lambda i: (0, 0)),
                  pl.BlockSpec((1, width), lambda i: (0, 0)),
                  pl.BlockSpec((1, dm), lambda i: (0, 0)),
                  pl.BlockSpec((tb, width), lambda i: (i, 0)),
                  pl.BlockSpec((tb, width), lambda i: (i, 0))],
        out_specs=[pl.BlockSpec((n_steps, tb, dm), lambda i: (0, i, 0)),
                   pl.BlockSpec((tb, width), lambda i: (i, 0)),
                   pl.BlockSpec((tb, width), lambda i: (i, 0))],
        out_shape=[jax.ShapeDtypeStruct((n_steps, nb, dm), BF16), st, st],
        scratch_shapes=[pltpu.VMEM((tb, width), F32), pltpu.VMEM((tb, width), F32)],
        compiler_params=_cparams(1),
        name="s5_sample",
    )(u, wb, cc, a_re.reshape(1, width), a_im.reshape(1, width), d.reshape(1, dm), s0_re, s0_im)


def _topk_rows(s, k):
    n = s.shape[0]
    row = lax.broadcasted_iota(jnp.int32, s.shape, 0)
    vals, idxs = [], []
    for t in range(k):
        m = jnp.max(s, axis=0, keepdims=True)
        idx = jnp.min(jnp.where(s == m, row, n), axis=0, keepdims=True)
        vals.append(m)
        idxs.append(idx)
        if t + 1 < k:
            s = jnp.where(row == idx, -jnp.inf, s)
    return jnp.concatenate(vals, axis=0), jnp.concatenate(idxs, axis=0)


def _candidate_pairs(k):
    return [(i, j) for i in range(k) for j in range(k) if (i + 1) * (j + 1) <= k]


def _peer_router_kernel(x_ref, g_ref, wq_ref, keys_ref, h_ref, gt_ref, i1_ref, i2_ref, q_s,
                        *, n_heads, topk):
    x = x_ref[...]
    hb = (x * _rms_scale(x) * g_ref[...]).astype(BF16)
    h_ref[...] = hb
    q_s[...] = jnp.dot(hb, wq_ref[...], preferred_element_type=F32).astype(BF16)
    tm = x.shape[0]
    n_keys, half = keys_ref.shape[1], keys_ref.shape[2]
    n_chunks = tm // LANES
    pairs = _candidate_pairs(topk)
    counts = [sum(1 for p in pairs if p[0] == i) for i in range(topk)]
    n_cand = _round_up(len(pairs), SUBLANES)

    def body(it, carry):
        hd = it % n_heads
        r0 = pl.multiple_of((it // n_heads) * LANES, LANES)
        q = q_s[pl.ds(r0, LANES), pl.ds(pl.multiple_of(hd * 2 * half, 2 * half), 2 * half)]
        s0 = lax.dot_general(keys_ref[0], q[:, :half], NT_DIMS, preferred_element_type=F32)
        s1 = lax.dot_general(keys_ref[1], q[:, half:], NT_DIMS, preferred_element_type=F32)
        sv0, si0 = _topk_rows(s0, topk)
        sv1, si1 = _topk_rows(s1, topk)
        si0 = si0.astype(F32) * float(n_keys)
        si1 = si1.astype(F32)
        cs, ce = [], []
        for i in range(topk):
            cnt = counts[i]
            cs.append(sv0[i:i + 1] + sv1[0:cnt])
            ce.append(si0[i:i + 1] + si1[0:cnt])
        pad = n_cand - len(pairs)
        if pad:
            cs.append(jnp.full((pad, LANES), -jnp.inf, F32))
            ce.append(jnp.zeros((pad, LANES), F32))
        cand = jnp.concatenate(cs, axis=0)
        cand_e = jnp.concatenate(ce, axis=0)
        row = lax.broadcasted_iota(jnp.int32, cand.shape, 0)
        ts, te = [], []
        for t in range(topk):
            m = jnp.max(cand, axis=0, keepdims=True)
            idx = jnp.min(jnp.where(cand == m, row, n_cand), axis=0, keepdims=True)
            sel = row == idx
            ts.append(m)
            te.append(jnp.sum(jnp.where(sel, cand_e, 0.0), axis=0, keepdims=True))
            if t + 1 < topk:
                cand = jnp.where(sel, -jnp.inf, cand)
        ts = jnp.concatenate(ts, axis=0)
        te = jnp.concatenate(te, axis=0)
        e = jnp.exp(ts - ts[0:1])
        gate = e / jnp.sum(e, axis=0, keepdims=True)
        t1 = jnp.floor(te * (1.0 / n_keys))
        rows = pl.ds(pl.multiple_of(hd * topk, topk), topk)
        cols = pl.ds(r0, LANES)
        gt_ref[rows, cols] = gate
        i1_ref[rows, cols] = t1
        i2_ref[rows, cols] = te - t1 * float(n_keys)
        return carry

    lax.fori_loop(0, n_heads * n_chunks, body, 0, unroll=8)


def _peer_router(x, g, w_q, sub_keys):
    n, d = x.shape
    n_keys, half = sub_keys.shape[1], sub_keys.shape[2]
    assert n_keys == LANES and half == LANES
    n_heads = w_q.shape[1] // (2 * half)
    hk = n_heads * PEER_TOPK
    tm = _pick_tile(n, (640, 512, 256, 128))
    kern = functools.partial(_peer_router_kernel, n_heads=n_heads, topk=PEER_TOPK)
    tr = jax.ShapeDtypeStruct((hk, n), F32)
    return pl.pallas_call(
        kern, grid=(n // tm,),
        in_specs=[pl.BlockSpec((tm, d), lambda i: (i, 0)),
                  pl.BlockSpec((1, d), lambda i: (0, 0)),
                  pl.BlockSpec(w_q.shape, lambda i: (0, 0)),
                  pl.BlockSpec(sub_keys.shape, lambda i: (0, 0, 0))],
        out_specs=[pl.BlockSpec((tm, d), lambda i: (i, 0)),
                   pl.BlockSpec((hk, tm), lambda i: (0, i)),
                   pl.BlockSpec((hk, tm), lambda i: (0, i)),
                   pl.BlockSpec((hk, tm), lambda i: (0, i))],
        out_shape=[jax.ShapeDtypeStruct((n, d), BF16), tr, tr, tr],
        scratch_shapes=[pltpu.VMEM((tm, w_q.shape[1]), BF16)],
        compiler_params=_cparams(1),
        name="peer_router",
    )(x, g.reshape(1, d), w_q, sub_keys)


def _peer_expand_kernel(gt_ref, i1_ref, i2_ref, *rest, pitch, cast_tables):
    if cast_tables:
        u_ref, v_ref, w_ref, ub_ref, vb_ref, g_s, a_s, b_s, scr = rest
        ub_ref[...] = u_ref[...].astype(BF16)
        vb_ref[...] = v_ref[...].astype(BF16)
    else:
        w_ref, g_s, a_s, b_s, scr = rest
    g_s[...] = gt_ref[...].T
    a_s[...] = i1_ref[...].T
    b_s[...] = i2_ref[...].T
    tn, hk = g_s.shape
    n_keys = w_ref.shape[1] // LANES
    sub = lax.broadcasted_iota(jnp.int32, (n_keys, hk), 0).astype(F32)

    def body(n, carry):
        at = jnp.where(a_s[pl.ds(n, 1), :] == sub, g_s[pl.ds(n, 1), :], 0.0).astype(BF16)
        bt = jnp.where(b_s[pl.ds(n, 1), :] == sub, 1.0, 0.0).astype(BF16)
        scr[pl.ds(pl.multiple_of(n * pitch, SUBLANES), n_keys), :] = lax.dot_general(
            at, bt, NT_DIMS, preferred_element_type=F32)
        return carry

    lax.fori_loop(0, tn, body, 0, unroll=32)
    for a in range(n_keys):
        w_ref[:, a * LANES:(a + 1) * LANES] = scr[pl.ds(a, tn, stride=pitch), :].astype(w_ref.dtype)


def _peer_expand(gt, i1, i2, n_keys, tables=None):
    hk, n = gt.shape
    assert n_keys == LANES and hk == LANES
    tn = LANES
    steps = n // tn
    pitch = n_keys + SUBLANES
    kern = functools.partial(_peer_expand_kernel, pitch=pitch, cast_tables=tables is not None)
    spec = pl.BlockSpec((hk, tn), lambda i: (0, i))
    in_specs = [spec, spec, spec]
    out_specs = [pl.BlockSpec((tn, n_keys * n_keys), lambda i: (i, 0))]
    out_shape = [jax.ShapeDtypeStruct((n, n_keys * n_keys), BF16)]
    args = [gt, i1, i2]
    if tables is not None:
        layer, u_all, v_all = tables
        _, n_exp, d = u_all.shape
        nb = 1 << (steps.bit_length() - 1)
        assert n_exp % nb == 0
        tr = n_exp // nb
        in_specs += [pl.BlockSpec((None, tr, d), lambda i: (layer, jnp.minimum(i, nb - 1), 0))] * 2
        out_specs += [pl.BlockSpec((tr, d), lambda i: (jnp.minimum(i, nb - 1), 0))] * 2
        out_shape += [jax.ShapeDtypeStruct((n_exp, d), BF16)] * 2
        args += [u_all, v_all]
    return pl.pallas_call(
        kern, grid=(steps,),
        in_specs=in_specs, out_specs=out_specs, out_shape=out_shape,
        scratch_shapes=[pltpu.VMEM((tn, hk), F32), pltpu.VMEM((tn, hk), F32), pltpu.VMEM((tn, hk), F32),
                        pltpu.VMEM((tn * pitch, LANES), F32)],
        compiler_params=_cparams(1),
        name="peer_expand",
    )(*args)


def _peer_dense_kernel(h_ref, u_ref, v_ref, w_ref, r_ref, o_ref):
    @pl.when(pl.program_id(1) == 0)
    def _():
        o_ref[...] = r_ref[...]

    act = lax.dot_general(h_ref[...], u_ref[...], NT_DIMS, preferred_element_type=F32)
    p = (w_ref[...].astype(F32) * _gelu(act)).astype(BF16)
    o_ref[...] += jnp.dot(p, v_ref[...], preferred_element_type=F32)


def _peer_dense(h, u_tab, v_tab, w, res):
    n, d = h.shape
    n_exp = u_tab.shape[0]
    tm = _pick_tile(n, (640, 512, 256, 128))
    te = 1024
    once = pl.Buffered(1)
    return pl.pallas_call(
        _peer_dense_kernel, grid=(n // tm, n_exp // te),
        in_specs=[pl.BlockSpec((tm, d), lambda i, e: (i, 0), pipeline_mode=once),
                  pl.BlockSpec((te, d), lambda i, e: (e, 0)),
                  pl.BlockSpec((te, d), lambda i, e: (e, 0)),
                  pl.BlockSpec((tm, te), lambda i, e: (i, e)),
                  pl.BlockSpec((tm, d), lambda i, e: (i, 0), pipeline_mode=once)],
        out_specs=pl.BlockSpec((tm, d), lambda i, e: (i, 0)),
        out_shape=jax.ShapeDtypeStruct((n, d), F32),
        compiler_params=_cparams(2),
        name="peer_dense",
    )(h, u_tab, v_tab, w, res)


def _peer_ffn(x, g, w_q, sub_keys, tables, tables_are_f32):
    h, gt, i1, i2 = _peer_router(x, g, w_q, sub_keys)
    if tables_are_f32:
        w, u_tab, v_tab = _peer_expand(gt, i1, i2, sub_keys.shape[1], tables)
    else:
        (w,) = _peer_expand(gt, i1, i2, sub_keys.shape[1])
        u_tab, v_tab = tables
    return _peer_dense(h, u_tab, v_tab, w, x), (u_tab, v_tab)


def kernel(x_prompt, x_sample, cache_k, cache_v, state_ssm_re, state_ssm_im, page_table, meta_tokens, norm_mix, norm_ffn, norm_final, attn_w_qkv, attn_lambda, attn_subln, attn_w_o, ssm_w_in, ssm_lambda_re, ssm_lambda_im, ssm_log_dt, ssm_b_re, ssm_b_im, ssm_c_re, ssm_c_im, ssm_d, ssm_w_glu, peer_w_q, peer_sub_keys, peer_u, peer_v):
    batch, seq, d_model = x_prompt.shape
    assert batch == 1
    dec_batch, dec_seq, _ = x_sample.shape
    n_meta = meta_tokens.shape[0]
    depth = norm_mix.shape[0]
    n_heads, hw = cache_k.shape[3], cache_k.shape[4]
    hd = hw // 2
    n_groups, state_dim = ssm_lambda_re.shape[1], ssm_lambda_re.shape[2]
    t_tot = n_meta + seq
    tq = 640
    tt = 160
    n_p = _round_up(t_tot, math.lcm(tq, tt))
    n_s = dec_batch * dec_seq
    slopes = jnp.asarray(2.0 ** (-8.0 * np.arange(1, n_heads + 1) / n_heads) * LOG2E, dtype=F32)

    xp = jnp.concatenate([meta_tokens.astype(F32), x_prompt[0],
                          jnp.zeros((n_p - t_tot, d_model), F32)], axis=0)
    xs = x_sample.reshape(n_s, d_model)

    outs = {k: [] for k in ("kp", "vp", "ks", "vs", "srp", "sip", "srs", "sis")}
    for i in range(depth):
        j = i // 2
        if i % 2 == 0:
            lam_init = _lambda_init(i)
            w_qkv = attn_w_qkv[j].astype(BF16)
            w_o = attn_w_o[j].astype(BF16)
            qk_w = n_heads * hw
            qkv_p, k_p, v_p = _qkv_matmul(xp, norm_mix[i], w_qkv, t_tot)
            qkv_s, k_s, v_s = _qkv_matmul(xs, norm_mix[i], w_qkv, n_s)
            o_p = _attn_prompt(qkv_p, attn_lambda[j], attn_subln[j], slopes, n_heads, hd, lam_init, tq)

            def rows(a):
                return a.reshape(dec_batch, dec_seq * n_heads, hw)

            o_s = _attn_sample(rows(qkv_s[:, :qk_w]), rows(k_s), rows(v_s),
                               cache_k, cache_v, j, page_table, attn_lambda[j], attn_subln[j], slopes,
                               n_heads, hd, lam_init)
            xp = _matmul_residual(o_p, w_o, xp)
            xs = _matmul_residual(o_s.reshape(n_s, qk_w), w_o, xs)
            outs["kp"].append(k_p.reshape(batch, t_tot, n_heads, hw))
            outs["vp"].append(v_p.reshape(batch, t_tot, n_heads, hw))
            outs["ks"].append(k_s.reshape(dec_batch, dec_seq, n_heads, hw))
            outs["vs"].append(v_s.reshape(dec_batch, dec_seq, n_heads, hw))
        else:
            w_in = ssm_w_in[j].astype(BF16)
            w_glu = ssm_w_glu[j].astype(BF16)
            a_re, a_im, wb, cc = _s5_weights(ssm_lambda_re[j], ssm_lambda_im[j], ssm_log_dt[j],
                                             ssm_b_re[j], ssm_b_im[j], ssm_c_re[j], ssm_c_im[j])
            d_vec = ssm_d[j].reshape(-1)
            u_p = _norm_matmul(xp, norm_mix[i], w_in)
            y_p, sr_p, si_p = _s5_prompt(u_p, a_re, a_im, wb, cc, d_vec, t_tot, tt)
            xp = _glu_residual(y_p, w_glu, xp)
            u_s = _norm_matmul(xs, norm_mix[i], w_in)
            u_s = u_s.reshape(dec_batch, dec_seq, d_model).transpose(1, 0, 2)
            y_s, sr_s, si_s = _s5_sample(u_s, a_re, a_im, wb, cc, d_vec,
                                         state_ssm_re[j].reshape(dec_batch, -1),
                                         state_ssm_im[j].reshape(dec_batch, -1))
            xs = _glu_residual(y_s.transpose(1, 0, 2).reshape(n_s, d_model), w_glu, xs)
            outs["srp"].append(sr_p.reshape(batch, n_groups, state_dim))
            outs["sip"].append(si_p.reshape(batch, n_groups, state_dim))
            outs["srs"].append(sr_s.reshape(dec_batch, n_groups, state_dim))
            outs["sis"].append(si_s.reshape(dec_batch, n_groups, state_dim))
        w_q = peer_w_q[i].astype(BF16)
        keys = peer_sub_keys[i].astype(BF16)
        xp, tables = _peer_ffn(xp, norm_ffn[i], w_q, keys, (i, peer_u, peer_v), True)
        xs, _ = _peer_ffn(xs, norm_ffn[i], w_q, keys, tables, False)

    y_prompt = _final_norm(xp, norm_final)[n_meta:t_tot].reshape(batch, seq, d_model)
    y_sample = _final_norm(xs, norm_final).reshape(dec_batch, dec_seq, d_model)
    return (y_prompt, y_sample,
            jnp.stack(outs["kp"]), jnp.stack(outs["vp"]), jnp.stack(outs["srp"]), jnp.stack(outs["sip"]),
            jnp.stack(outs["ks"]), jnp.stack(outs["vs"]), jnp.stack(outs["srs"]), jnp.stack(outs["sis"]))
```
